```python
import jax
import jax.numpy as jnp
from jax import lax
import numpy as np

D_MODEL = 1024
BATCH = 4
SEQ = 4096
DEPTH = 2
DEC_BATCH = 128
DEC_SEQ = 8
PAST_LEN = 2048
PAGE_SIZE = 128

N_EVEN = (DEPTH + 1) // 2
N_ODD = DEPTH // 2
EPS = 1e-6
F32 = jnp.float32
A_HEADS = 4
A_DK = 128
A_DV = 128
A_CONV = 4
A_CHUNK = 64
A_QK = A_HEADS * A_DK
A_V = A_HEADS * A_DV
A_CONV_DIM = 2 * A_QK + A_V
A_IN = A_CONV_DIM + A_V + 2 * A_HEADS
B_WIDTH = D_MODEL // 2
B_CONV = 3
B_IN = 3 * B_WIDTH
C_HEADS = 8
C_KV_HEADS = 2
C_HD = 64
C_CMP_STRIDE = 16
C_CMP_LEN = 2 * C_CMP_STRIDE
C_CMP_HID = 2 * C_HD
C_SEL_LEN = 64
C_TOPN = 16
C_WINDOW = 512
C_QBLK = 128
C_Q = C_HEADS * C_HD
C_GATES = 3 * C_HEADS
C_IN = C_Q + C_GATES + 3 * 2 * C_KV_HEADS * C_HD
D_GROUPS = 8
D_WIDTH = D_MODEL // 2
D_CHUNK = 128
D_IN = 2 * D_WIDTH
D_FF = 2816
EVEN_IN = A_IN + B_IN
EVEN_OUT = A_V + B_WIDTH
ODD_IN = C_IN + D_IN
ODD_OUT = C_Q + D_WIDTH
BIG = 1e4
NEG = -1e30

kernel_name = 'hybrid_gdn_shortconv_nsa_chunkmlp_step'


def rmsnorm(x, g):
    xf = x.astype(F32)
    y = xf * lax.rsqrt(jnp.mean(xf * xf, axis=-1, keepdims=True) + EPS)
    return (y * g.astype(F32)).astype(x.dtype)


def l2norm(x):
    x = x.astype(F32)
    return x * lax.rsqrt(jnp.sum(x * x, axis=-1, keepdims=True) + EPS)


def swiglu(x, w_gate, w_up, w_down):
    return (jax.nn.silu(x @ w_gate) * (x @ w_up)) @ w_down


def masked_softmax(s, mask):
    s = jnp.where(mask, s, NEG)
    m = jnp.max(s, axis=-1, keepdims=True)
    e = jnp.where(mask, jnp.exp(s - m), 0.0)
    return e / jnp.maximum(jnp.sum(e, axis=-1, keepdims=True), 1e-30)


def alibi_slopes():
    return 2.0 ** (-8.0 * jnp.arange(1, C_HEADS + 1, dtype=F32) / C_HEADS)


def causal_dwconv(x, buf, w):
    T = x.shape[1]
    K = w.shape[0]
    xp = jnp.concatenate([buf.astype(x.dtype), x], axis=1)
    y = sum(xp[:, i:i + T] * w[i] for i in range(K))
    return y, xp[:, T:]


def gated_delta_rule(q, k, v, g, beta, s0):
    B, T, H, _ = q.shape
    C = A_CHUNK if T >= A_CHUNK else T
    pad = (-T) % C
    padt = lambda a: jnp.pad(a, [(0, 0), (0, pad)] + [(0, 0)] * (a.ndim - 2))
    N = (T + pad) // C
    chunks = lambda a: jnp.moveaxis(padt(a).astype(F32).reshape((B, N, C) + a.shape[2:]), 3, 1)
    qc, kc, vc = chunks(q), chunks(k), chunks(v)
    gc = jnp.cumsum(chunks(g), axis=-1)
    bc = chunks(beta)
    tri_incl = jnp.tril(jnp.ones((C, C), bool))
    tri_strict = jnp.tril(jnp.ones((C, C), bool), -1)
    dmask = jnp.where(tri_incl, jnp.exp(jnp.where(tri_incl, gc[..., :, None] - gc[..., None, :], 0.0)), 0.0)
    kb = kc * bc[..., None]
    vb = vc * bc[..., None]
    a_mat = jnp.where(tri_strict, jnp.einsum('bhncd,bhnsd->bhncs', kb, kc) * dmask, 0.0)
    eye = jnp.eye(C, dtype=F32)
    t_mat = lax.linalg.triangular_solve(eye + a_mat, jnp.broadcast_to(eye, a_mat.shape),
                                        left_side=True, lower=True, unit_diagonal=True)
    u = jnp.einsum('bhncs,bhnsd->bhncd', t_mat, vb)
    w = jnp.einsum('bhncs,bhnsd->bhncd', t_mat, kb * jnp.exp(gc)[..., None])
    intra = jnp.where(tri_incl, jnp.einsum('bhncd,bhnsd->bhncs', qc, kc) * dmask, 0.0)
    g_last = gc[..., -1]
    k_tail = kc * jnp.exp(g_last[..., None] - gc)[..., None]
    q_dec = qc * jnp.exp(gc)[..., None]

    def step(s, inp):
        w_n, u_n, qd_n, in_n, kt_n, gl_n = inp
        v_new = u_n - jnp.einsum('bhcd,bhde->bhce', w_n, s)
        o = jnp.einsum('bhcd,bhde->bhce', qd_n, s) + jnp.einsum('bhcs,bhse->bhce', in_n, v_new)
        s = s * jnp.exp(gl_n)[..., None, None] + jnp.einsum('bhcd,bhce->bhde', kt_n, v_new)
        return s, o

    xs = tuple(jnp.moveaxis(a, 2, 0) for a in (w, u, q_dec, intra, k_tail, g_last))
    s_fin, o = lax.scan(step, s0.astype(F32), xs)
    o = jnp.transpose(o, (1, 0, 3, 2, 4)).reshape(B, N * C, H, -1)[:, :T]
    return o, s_fin


def gdn_mixer(p, conv_buf, s0, w_conv, a_log, dt_bias, g_norm):
    B, T, _ = p.shape
    qkv, conv_new = causal_dwconv(p[..., :A_CONV_DIM], conv_buf, w_conv)
    qkv = jax.nn.silu(qkv.astype(F32))
    q = l2norm(qkv[..., :A_QK].reshape(B, T, A_HEADS, A_DK)) * (A_DK ** -0.5)
    k = l2norm(qkv[..., A_QK:2 * A_QK].reshape(B, T, A_HEADS, A_DK))
    v = qkv[..., 2 * A_QK:].reshape(B, T, A_HEADS, A_DV)
    z = p[..., A_CONV_DIM:A_CONV_DIM + A_V].reshape(B, T, A_HEADS, A_DV).astype(F32)
    b = p[..., A_CONV_DIM + A_V:A_CONV_DIM + A_V + A_HEADS].astype(F32)
    a = p[..., A_CONV_DIM + A_V + A_HEADS:].astype(F32)
    beta = jax.nn.sigmoid(b)
    g = -jnp.exp(a_log.astype(F32)) * jax.nn.softplus(a + dt_bias.astype(F32))
    o, s = gated_delta_rule(q, k, v, g, beta, s0)
    o = rmsnorm(o, g_norm) * jax.nn.silu(z)
    return o.reshape(B, T, A_V), s, conv_new


def shortconv_mixer(p, conv_buf, w_conv):
    h = p[..., :B_WIDTH]
    gate_b = p[..., B_WIDTH:2 * B_WIDTH]
    gate_c = p[..., 2 * B_WIDTH:]
    y, conv_new = causal_dwconv(gate_c * h, conv_buf, w_conv)
    return gate_b * y, conv_new


def even_mix(h, s0, conv_a0, conv_b0, w_in, w_out, a_conv, a_log, dt_bias, a_norm, b_conv):
    p = h @ w_in
    o_a, s, conv_a = gdn_mixer(p[..., :A_IN], conv_a0, s0, a_conv, a_log, dt_bias, a_norm)
    o_b, conv_b = shortconv_mixer(p[..., A_IN:], conv_b0, b_conv)
    y = jnp.concatenate([o_a.astype(h.dtype), o_b.astype(h.dtype)], axis=-1) @ w_out
    return y, s, conv_a, conv_b


def compress(rows, pe, w1, w2):
    B, Tk, G, dh = rows.shape
    S = C_CMP_STRIDE
    nh = Tk // S
    halves = rows[:, :nh * S].reshape(B, nh, S, G, dh).astype(F32)
    pa = jnp.einsum('bnsgd,sdh->bngh', halves + pe[:S, None, :], w1[:S])
    pb = jnp.einsum('bnsgd,sdh->bngh', halves + pe[S:, None, :], w1[S:])
    hid = jax.nn.silu(pa[:, :-1] + pb[:, 1:])
    return jnp.einsum('bngh,hd->bgnd', hid, w2)


def cmp_to_sel(nc, ns):
    cs = C_CMP_STRIDE * jnp.arange(nc)[:, None]
    ss = C_SEL_LEN * jnp.arange(ns)[None, :]
    ov = jnp.clip(jnp.minimum(cs + C_CMP_LEN, ss + C_SEL_LEN) - jnp.maximum(cs, ss), 0, None)
    return ov.astype(F32) / C_CMP_LEN


def nsa_attend(q, gates, q_pos, k_cmp, v_cmp, cmp_end, m_cs, k_selb, v_selb, k_win, v_win, w_pos, w_lo):
    B, Tq = q.shape[:2]
    G, R, dh = C_KV_HEADS, C_HEADS // C_KV_HEADS, C_HD
    qg = jnp.transpose(q.reshape(B, Tq, G, R, dh), (0, 2, 1, 3, 4)).astype(F32) * (dh ** -0.5)
    slopes = alibi_slopes().reshape(G, R)
    qp = q_pos.astype(F32)
    mask_c = cmp_end[None, :] <= q_pos[:, None]
    dist_c = qp[:, None] - cmp_end.astype(F32)[None, :]
    s_c = jnp.einsum('bgtrd,bgnd->bgtrn', qg, k_cmp) - slopes[:, None, :, None] * dist_c[None, :, None, :]
    p_c = masked_softmax(s_c, mask_c[None, None, :, None, :])
    o_c = jnp.einsum('bgtrn,bgnd->bgtrd', p_c, v_cmp)
    ns = k_selb.shape[2]
    n_top = min(C_TOPN, ns)
    imp = jnp.einsum('bgtrn,ns->bgts', p_c, m_cs)
    blk = jnp.arange(ns)[None, :]
    cur = (q_pos // C_SEL_LEN)[:, None]
    valid = blk <= cur
    forced = (blk == 0) | (blk == cur) | (blk == cur - 1)
    score = jnp.where(valid, jnp.where(forced, BIG, imp), NEG)
    top_val, top_idx = lax.top_k(score, n_top)
    picked = top_val > 0.5 * NEG
    take = jax.vmap(jax.vmap(lambda blocks, idx: blocks[idx]))
    k_s = take(k_selb, top_idx).reshape(B, G, Tq, n_top * C_SEL_LEN, dh)
    v_s = take(v_selb, top_idx).reshape(B, G, Tq, n_top * C_SEL_LEN, dh)
    pos_s = top_idx[..., None] * C_SEL_LEN + jnp.arange(C_SEL_LEN)
    mask_s = (picked[..., None] & (pos_s <= q_pos[None, None, :, None, None])).reshape(B, G, Tq, 1, -1)
    dist_s = (qp[None, None, :, None, None] - pos_s.astype(F32)).reshape(B, G, Tq, 1, -1)
    s_s = jnp.einsum('bgtrd,bgtkd->bgtrk', qg, k_s.astype(F32)) - slopes[None, :, None, :, None] * dist_s
    p_s = masked_softmax(s_s, mask_s)
    o_s = jnp.einsum('bgtrk,bgtkd->bgtrd', p_s, v_s.astype(F32))
    dist_w = qp[:, None] - w_pos.astype(F32)[None, :]
    mask_w = ((w_pos[None, :] <= q_pos[:, None]) & (q_pos[:, None] - w_pos[None, :] < C_WINDOW)
              & (w_pos[None, :] >= w_lo))
    s_w = jnp.einsum('bgtrd,bgkd->bgtrk', qg, k_win.astype(F32)) - slopes[:, None, :, None] * dist_w[None, :, None, :]
    p_w = masked_softmax(s_w, mask_w[None, None, :, None, :])
    o_w = jnp.einsum('bgtrk,bgkd->bgtrd', p_w, v_win.astype(F32))
    gt = jnp.transpose(gates.reshape(B, Tq, G, R, 3), (0, 2, 1, 3, 4))
    o = gt[..., 0:1] * o_c + gt[..., 1:2] * o_s + gt[..., 2:3] * o_w
    return jnp.transpose(o, (0, 2, 1, 3, 4)).reshape(B, Tq, C_HEADS * dh)


def nsa_core(q, gates, q_start, cmp_all, sel_all, win_all, win_start, pe, w1, w2):
    B, Tq = q.shape[:2]
    Tk = cmp_all.shape[1]
    k_cmp = compress(cmp_all[:, :, 0], pe[0], w1[0], w2[0])
    v_cmp = compress(cmp_all[:, :, 1], pe[1], w1[1], w2[1])
    nc = k_cmp.shape[2]
    cmp_end = C_CMP_STRIDE * jnp.arange(nc) + C_CMP_LEN - 1
    ns = -(-Tk // C_SEL_LEN)
    selp = jnp.pad(sel_all, ((0, 0), (0, ns * C_SEL_LEN - Tk), (0, 0), (0, 0), (0, 0)))
    selp = selp.reshape(B, ns, C_SEL_LEN, 2, C_KV_HEADS, C_HD)
    k_selb = jnp.transpose(selp[:, :, :, 0], (0, 3, 1, 2, 4))
    v_selb = jnp.transpose(selp[:, :, :, 1], (0, 3, 1, 2, 4))
    m_cs = cmp_to_sel(nc, ns)
    k_win = jnp.transpose(win_all[:, :, 0], (0, 2, 1, 3))
    v_win = jnp.transpose(win_all[:, :, 1], (0, 2, 1, 3))
    shared = (k_cmp, v_cmp, cmp_end, m_cs, k_selb, v_selb)
    if Tq <= C_QBLK:
        q_pos = q_start + jnp.arange(Tq)
        w_pos = win_start + jnp.arange(k_win.shape[2])
        return nsa_attend(q, gates, q_pos, *shared, k_win, v_win, w_pos, win_start)
    off = q_start - win_start
    pad_w = ((0, 0), (0, 0), (C_WINDOW, 0), (0, 0))
    kwp = jnp.pad(k_win, pad_w)
    vwp = jnp.pad(v_win, pad_w)
    nkw = C_WINDOW + C_QBLK

    def block(i):
        q0 = i * C_QBLK
        qb = lax.dynamic_slice_in_dim(q, q0, C_QBLK, axis=1)
        gb = lax.dynamic_slice_in_dim(gates, q0, C_QBLK, axis=1)
        kw = lax.dynamic_slice_in_dim(kwp, off + q0, nkw, axis=2)
        vw = lax.dynamic_slice_in_dim(vwp, off + q0, nkw, axis=2)
        q_pos = q_start + q0 + jnp.arange(C_QBLK)
        w_pos = q_start + q0 - C_WINDOW + jnp.arange(nkw)
        return nsa_attend(qb, gb, q_pos, *shared, kw, vw, w_pos, win_start)

    out = lax.map(block, jnp.arange(Tq // C_QBLK))
    return jnp.moveaxis(out, 0, 1).reshape(B, Tq, C_HEADS * C_HD)


def chunk_mlp(p, ws, bs, ln_g, ln_b):
    B, T, _ = p.shape
    z = jax.nn.gelu(p.astype(F32))
    u, v = z[..., :D_WIDTH], z[..., D_WIDTH:]
    mu = jnp.mean(v, axis=-1, keepdims=True)
    var = jnp.mean(jnp.square(v - mu), axis=-1, keepdims=True)
    v = (v - mu) * lax.rsqrt(var + EPS) * ln_g.astype(F32) + ln_b.astype(F32)
    pad = (-T) % D_CHUNK
    nch = (T + pad) // D_CHUNK
    vc = jnp.pad(v, ((0, 0), (0, pad), (0, 0))).reshape(B, nch, D_CHUNK, D_GROUPS, D_WIDTH // D_GROUPS)
    w_causal = jnp.where(jnp.tril(jnp.ones((D_CHUNK, D_CHUNK), bool)), ws.astype(F32), 0.0)
    mixed = jnp.einsum('gts,bnsgc->bntgc', w_causal, vc) + bs.astype(F32).T[None, None, :, :, None]
    mixed = mixed.reshape(B, nch * D_CHUNK, D_WIDTH)[:, :T]
    return u * mixed, v.astype(p.dtype)


def odd_mix(h, past_cmp, past_sel, win_buf, q_start, w_in, w_out, pe, w1, w2, ws, bs, ln_g, ln_b):
    B, T, _ = h.shape
    p = h @ w_in
    pc = p[..., :C_IN]
    q = pc[..., :C_Q].reshape(B, T, C_HEADS, C_HD)
    gates = jax.nn.sigmoid(pc[..., C_Q:C_Q + C_GATES].astype(F32)).reshape(B, T, C_HEADS, 3)
    kv = pc[..., C_Q + C_GATES:].reshape(B, T, 3, 2, C_KV_HEADS, C_HD)
    cmp_new, sel_new, win_new = kv[:, :, 0], kv[:, :, 1], kv[:, :, 2]
    cmp_all = jnp.concatenate([past_cmp.astype(h.dtype), cmp_new], axis=1)
    sel_all = jnp.concatenate([past_sel.astype(h.dtype), sel_new], axis=1)
    win_all = jnp.concatenate([win_buf.astype(h.dtype), win_new], axis=1)
    win_start = q_start - win_buf.shape[1]
    o_c = nsa_core(q, gates, q_start, cmp_all, sel_all, win_all, win_start, pe, w1, w2)
    o_d, v_rows = chunk_mlp(p[..., C_IN:], ws, bs, ln_g, ln_b)
    y = jnp.concatenate([o_c.astype(h.dtype), o_d.astype(h.dtype)], axis=-1) @ w_out
    n_keep = min(C_WINDOW, q_start + T)
    win_keep = win_all[:, win_all.shape[1] - n_keep:]
    return y, cmp_new, sel_new, win_keep, v_rows


def gather_paged(pool, page_table):
    pages = jnp.moveaxis(pool[page_table], 2, 1)
    db, nl, npg, pg = pages.shape[:4]
    return pages.reshape((db, nl, npg * pg) + pages.shape[4:])


def trunk(x, q_start, delta0, conv_a0, conv_b0, past_cmp, past_sel, win_buf,
          norm_g, ffn_gate, ffn_up, ffn_down,
          ev_w_in, ev_w_out, ev_a_conv, ev_a_log, ev_dt_bias, ev_a_norm, ev_b_conv,
          od_w_in, od_w_out, od_cmp_pe, od_cmp_w1, od_cmp_w2, od_d_ws, od_d_bs, od_d_ln_g, od_d_ln_b):
    deltas, convs_a, convs_b = [], [], []
    cmps, sels, wins, dvs = [], [], [], []
    for li in range(DEPTH):
        j = li // 2
        g = norm_g[li]
        x = x + 0.5 * rmsnorm(swiglu(rmsnorm(x, g[0]), ffn_gate[li, 0], ffn_up[li, 0], ffn_down[li, 0]), g[1])
        h = rmsnorm(x, g[2])
        if li % 2 == 0:
            y, s, ca, cb = even_mix(h, delta0[:, j], conv_a0[:, j], conv_b0[:, j], ev_w_in[j], ev_w_out[j],
                                    ev_a_conv[j], ev_a_log[j], ev_dt_bias[j], ev_a_norm[j], ev_b_conv[j])
            deltas.append(s)
            convs_a.append(ca)
            convs_b.append(cb)
        else:
            y, rc, rs, wb, dv = odd_mix(h, past_cmp[:, j], past_sel[:, j], win_buf[:, j], q_start,
                                        od_w_in[j], od_w_out[j], od_cmp_pe[j], od_cmp_w1[j], od_cmp_w2[j],
                                        od_d_ws[j], od_d_bs[j], od_d_ln_g[j], od_d_ln_b[j])
            cmps.append(rc)
            sels.append(rs)
            wins.append(wb)
            dvs.append(dv)
        x = x + rmsnorm(y, g[3])
        x = x + 0.5 * rmsnorm(swiglu(rmsnorm(x, g[4]), ffn_gate[li, 1], ffn_up[li, 1], ffn_down[li, 1]), g[5])
    st = lambda a: jnp.stack(a, axis=1)
    return x, st(deltas), st(convs_a), st(convs_b), st(cmps), st(sels), st(wins), st(dvs)


def setup_inputs(seed: int = 0) -> dict:
    key = jax.random.key(seed)
    ks = iter(jax.random.split(key, 32))
    nrm = lambda shape, scale: jax.random.normal(next(ks), shape, F32) * scale
    n_pages = PAST_LEN // PAGE_SIZE
    n_used = DEC_BATCH * n_pages
    n_pool = n_used + n_used // 4
    win_rows = min(C_WINDOW, PAST_LEN)
    kv_row = (2, C_KV_HEADS, C_HD)
    page_table = jax.random.permutation(next(ks), n_pool)[:n_used].reshape(DEC_BATCH, n_pages).astype(jnp.int32)
    return {
        'x_prompt': nrm((BATCH, SEQ, D_MODEL), 1.0),
        'x_sample': nrm((DEC_BATCH, DEC_SEQ, D_MODEL), 1.0),
        'state_delta': nrm((DEC_BATCH, N_EVEN, A_HEADS, A_DK, A_DV), 0.1),
        'state_conv_a': nrm((DEC_BATCH, N_EVEN, A_CONV - 1, A_CONV_DIM), 1.0),
        'state_conv_b': nrm((DEC_BATCH, N_EVEN, B_CONV - 1, B_WIDTH), 1.0),
        'cache_cmp_kv': nrm((n_pool, N_ODD, PAGE_SIZE) + kv_row, 1.0),
        'cache_sel_kv': nrm((n_pool, N_ODD, PAGE_SIZE) + kv_row, 1.0),
        'cache_win_kv': nrm((DEC_BATCH, N_ODD, win_rows) + kv_row, 1.0),
        'page_table': page_table,
        'norm_g': 1.0 + nrm((DEPTH, 6, D_MODEL), 0.01),
        'ffn_gate': nrm((DEPTH, 2, D_MODEL, D_FF), D_MODEL ** -0.5),
        'ffn_up': nrm((DEPTH, 2, D_MODEL, D_FF), D_MODEL ** -0.5),
        'ffn_down': nrm((DEPTH, 2, D_FF, D_MODEL), D_FF ** -0.5),
        'ev_w_in': nrm((N_EVEN, D_MODEL, EVEN_IN), D_MODEL ** -0.5),
        'ev_w_out': nrm((N_EVEN, EVEN_OUT, D_MODEL), EVEN_OUT ** -0.5),
        'ev_a_conv': nrm((N_EVEN, A_CONV, A_CONV_DIM), A_CONV ** -0.5),
        'ev_a_log': jnp.log(jax.random.uniform(next(ks), (N_EVEN, A_HEADS), F32, 1.0, 16.0)),
        'ev_dt_bias': -4.0 + nrm((N_EVEN, A_HEADS), 0.1),
        'ev_a_norm': 1.0 + nrm((N_EVEN, A_DV), 0.01),
        'ev_b_conv': nrm((N_EVEN, B_CONV, B_WIDTH), B_CONV ** -0.5),
        'od_w_in': nrm((N_ODD, D_MODEL, ODD_IN), D_MODEL ** -0.5),
        'od_w_out': nrm((N_ODD, ODD_OUT, D_MODEL), ODD_OUT ** -0.5),
        'od_cmp_pe': nrm((N_ODD, 2, C_CMP_LEN, C_HD), 0.02),
        'od_cmp_w1': nrm((N_ODD, 2, C_CMP_LEN, C_HD, C_CMP_HID), (C_CMP_LEN * C_HD) ** -0.5),
        'od_cmp_w2': nrm((N_ODD, 2, C_CMP_HID, C_HD), C_CMP_HID ** -0.5),
        'od_d_ws': nrm((N_ODD, D_GROUPS, D_CHUNK, D_CHUNK), D_CHUNK ** -0.5),
        'od_d_bs': 1.0 + nrm((N_ODD, D_GROUPS, D_CHUNK), 0.01),
        'od_d_ln_g': 1.0 + nrm((N_ODD, D_WIDTH), 0.01),
        'od_d_ln_b': nrm((N_ODD, D_WIDTH), 0.01),
    }


def reference(x_prompt, x_sample, state_delta, state_conv_a, state_conv_b, cache_cmp_kv, cache_sel_kv,
              cache_win_kv, page_table, norm_g, ffn_gate, ffn_up, ffn_down,
              ev_w_in, ev_w_out, ev_a_conv, ev_a_log, ev_dt_bias, ev_a_norm, ev_b_conv,
              od_w_in, od_w_out, od_cmp_pe, od_cmp_w1, od_cmp_w2, od_d_ws, od_d_bs, od_d_ln_g, od_d_ln_b):
    weights = (norm_g, ffn_gate, ffn_up, ffn_down,
               ev_w_in, ev_w_out, ev_a_conv, ev_a_log, ev_dt_bias, ev_a_norm, ev_b_conv,
               od_w_in, od_w_out, od_cmp_pe, od_cmp_w1, od_cmp_w2, od_d_ws, od_d_bs, od_d_ln_g, od_d_ln_b)
    dt = x_prompt.dtype
    bp = x_prompt.shape[0]
    no_rows = jnp.zeros((bp, N_ODD, 0, 2, C_KV_HEADS, C_HD), dt)
    (y_prompt, delta_p, conv_a_p, conv_b_p, cmp_p, sel_p, win_p, _) = trunk(
        x_prompt, 0,
        jnp.zeros((bp, N_EVEN, A_HEADS, A_DK, A_DV), F32),
        jnp.zeros((bp, N_EVEN, A_CONV - 1, A_CONV_DIM), dt),
        jnp.zeros((bp, N_EVEN, B_CONV - 1, B_WIDTH), dt),
        no_rows, no_rows, no_rows, *weights)
    past_len = page_table.shape[1] * cache_cmp_kv.shape[2]
    (y_sample, delta_s, conv_a_s, conv_b_s, cmp_s, sel_s, win_s, dv_s) = trunk(
        x_sample, past_len, state_delta, state_conv_a, state_conv_b,
        gather_paged(cache_cmp_kv, page_table), gather_paged(cache_sel_kv, page_table),
        cache_win_kv, *weights)
    return (y_prompt, y_sample, delta_p, delta_s, conv_a_p, conv_a_s, conv_b_p, conv_b_s,
            cmp_p, cmp_s, sel_p, sel_s, win_p, win_s, dv_s)
```

```python
import functools

import jax
import jax.numpy as jnp
from jax import lax
from jax.experimental import pallas as pl
from jax.experimental.pallas import tpu as pltpu

F32 = jnp.float32
BF16 = jnp.bfloat16

D_MODEL = 1024
D_FF = 2816
EPS = 1e-6
A_HEADS = 4
A_DK = 128
A_DV = 128
A_CONV = 4
A_CHUNK = 64
A_QK = A_HEADS * A_DK
A_V = A_HEADS * A_DV
A_CONV_DIM = 2 * A_QK + A_V
B_WIDTH = 512
B_CONV = 3
C_HEADS = 8
C_KV_HEADS = 2
C_REP = C_HEADS // C_KV_HEADS
C_HD = 64
C_CMP_STRIDE = 16
C_CMP_LEN = 32
C_CMP_HID = 128
C_SEL_LEN = 64
C_TOPN = 16
C_WINDOW = 512
C_Q = C_HEADS * C_HD
C_KV = 3 * 2 * C_KV_HEADS * C_HD
D_GROUPS = 8
D_WIDTH = 512
D_CHUNK = 128
BIG = 1e4
NEG = -1e30

LANES = 128
VMEM_LIMIT = 56 * 1024 * 1024


def _cparams(sem):
    return pltpu.CompilerParams(dimension_semantics=sem, vmem_limit_bytes=VMEM_LIMIT)


def _rms(x, g):
    return x * lax.rsqrt(jnp.mean(x * x, axis=-1, keepdims=True) + EPS) * g


def _mm(a, b):
    return jnp.dot(a.astype(BF16), b.astype(BF16), preferred_element_type=F32)


def _mm_nt(a, b):
    return lax.dot_general(a.astype(BF16), b.astype(BF16), (((1,), (1,)), ((), ())),
                           preferred_element_type=F32)


def _mm_tn(a, b):
    return lax.dot_general(a.astype(BF16), b.astype(BF16), (((0,), (0,)), ((), ())),
                           preferred_element_type=F32)


def _row_tile(n, pref):
    t = min(n, pref)
    while n % t:
        t //= 2
    return t


def _const_spec(shape):
    return pl.BlockSpec(shape, lambda *_: (0,) * len(shape))


def _ffn_kernel(x_ref, gpre_ref, gpost_ref, wg_ref, wu_ref, wd_ref, o_ref, *, ff_chunk):
    x = x_ref[...]
    h = _rms(x, gpre_ref[...]).astype(BF16)
    acc = jnp.zeros(x.shape, F32)
    for c in range(D_FF // ff_chunk):
        sl = slice(c * ff_chunk, (c + 1) * ff_chunk)
        a = jnp.dot(h, wg_ref[:, sl], preferred_element_type=F32)
        u = jnp.dot(h, wu_ref[:, sl], preferred_element_type=F32)
        act = (jax.nn.silu(a) * u).astype(BF16)
        acc = acc + jnp.dot(act, wd_ref[sl, :], preferred_element_type=F32)
    o_ref[...] = x + 0.5 * _rms(acc, gpost_ref[...])


def _ffn_half(x, g_pre, g_post, wg, wu, wd):
    n = x.shape[0]
    tm = _row_tile(n, 512)
    row = pl.BlockSpec((tm, D_MODEL), lambda i: (i, 0))
    return pl.pallas_call(
        functools.partial(_ffn_kernel, ff_chunk=D_FF // 2),
        grid=(n // tm,),
        in_specs=[row, _const_spec((1, D_MODEL)), _const_spec((1, D_MODEL)),
                  _const_spec((D_MODEL, D_FF)), _const_spec((D_MODEL, D_FF)), _const_spec((D_FF, D_MODEL))],
        out_specs=row,
        out_shape=jax.ShapeDtypeStruct((n, D_MODEL), F32),
        compiler_params=_cparams(("parallel",)),
        name="ffn_half",
    )(x, g_pre.reshape(1, -1), g_post.reshape(1, -1), wg, wu, wd)


def _proj_kernel(x_ref, g_ref, w_ref, o_ref):
    h = _rms(x_ref[...], g_ref[...]).astype(BF16)
    o_ref[...] = jnp.dot(h, w_ref[...], preferred_element_type=F32)


def _in_proj(x, g, w):
    n = x.shape[0]
    cols = w.shape[1]
    tm = _row_tile(n, 512)
    return pl.pallas_call(
        _proj_kernel,
        grid=(n // tm,),
        in_specs=[pl.BlockSpec((tm, D_MODEL), lambda i: (i, 0)), _const_spec((1, D_MODEL)),
                  _const_spec((D_MODEL, cols))],
        out_specs=pl.BlockSpec((tm, cols), lambda i: (i, 0)),
        out_shape=jax.ShapeDtypeStruct((n, cols), F32),
        compiler_params=_cparams(("parallel",)),
        name="in_proj",
    )(x, g.reshape(1, -1), w)


def _out_proj_kernel(x_ref, oa_ref, ob_ref, g_ref, w_ref, o_ref):
    half = oa_ref.shape[1]
    y = (jnp.dot(oa_ref[...].astype(BF16), w_ref[:half, :], preferred_element_type=F32)
         + jnp.dot(ob_ref[...].astype(BF16), w_ref[half:, :], preferred_element_type=F32))
    o_ref[...] = x_ref[...] + _rms(y, g_ref[...])


def _out_proj(x, oa, ob, g, w):
    n = x.shape[0]
    tm = _row_tile(n, 512)
    half = oa.shape[1]
    return pl.pallas_call(
        _out_proj_kernel,
        grid=(n // tm,),
        in_specs=[pl.BlockSpec((tm, D_MODEL), lambda i: (i, 0)),
                  pl.BlockSpec((tm, half), lambda i: (i, 0)), pl.BlockSpec((tm, half), lambda i: (i, 0)),
                  _const_spec((1, D_MODEL)), _const_spec((2 * half, D_MODEL))],
        out_specs=pl.BlockSpec((tm, D_MODEL), lambda i: (i, 0)),
        out_shape=jax.ShapeDtypeStruct((n, D_MODEL), F32),
        compiler_params=_cparams(("parallel",)),
        name="out_proj",
    )(x, oa, ob, g.reshape(1, -1), w)


EV_QKV = 0
EV_Z = A_CONV_DIM
EV_H = EV_Z + A_V
EV_GB = EV_H + B_WIDTH
EV_GC = EV_GB + B_WIDTH
EV_BA = EV_GC + B_WIDTH
EV_COLS = EV_BA + LANES
CARRY = 8


def _even_w_in(w):
    a_in = A_CONV_DIM + A_V + 2 * A_HEADS
    pad = jnp.zeros((w.shape[0], LANES - 2 * A_HEADS), w.dtype)
    return jnp.concatenate([w[:, :A_CONV_DIM + A_V], w[:, a_in:], w[:, A_CONV_DIM + A_V:a_in], pad], axis=1)


def _even_prep_kernel(qkv_ref, h_ref, gb_ref, gc_ref, ba_ref, ca0_ref, cb0_ref, wa_ref, wb_ref, alog_ref,
                      dtb_ref, q_ref, k_ref, v_ref, gbeta_ref, ob_ref, can_ref, cbn_ref, xa, xb, *, tt):
    j = pl.program_id(1)

    @pl.when(j == 0)
    def _():
        xa[CARRY - (A_CONV - 1):CARRY, :] = ca0_ref[...]
        xb[CARRY - (B_CONV - 1):CARRY, :] = cb0_ref[...]

    @pl.when(j > 0)
    def _():
        xa[0:CARRY, :] = xa[tt:tt + CARRY, :]
        xb[0:CARRY, :] = xb[tt:tt + CARRY, :]

    xa[CARRY:CARRY + tt, :] = qkv_ref[...]
    xb[CARRY:CARRY + tt, :] = gc_ref[...] * h_ref[...]

    ya = wa_ref[0:1, :] * xa[pl.ds(CARRY - 3, tt), :]
    for i in range(1, A_CONV):
        ya = ya + wa_ref[i:i + 1, :] * xa[pl.ds(CARRY - 3 + i, tt), :]
    yb = wb_ref[0:1, :] * xb[pl.ds(CARRY - 2, tt), :]
    for i in range(1, B_CONV):
        yb = yb + wb_ref[i:i + 1, :] * xb[pl.ds(CARRY - 2 + i, tt), :]

    ob_ref[...] = gb_ref[...] * yb
    can_ref[...] = xa[tt + CARRY - (A_CONV - 1):tt + CARRY, :]
    cbn_ref[...] = xb[tt + CARRY - (B_CONV - 1):tt + CARRY, :]

    ya = jax.nn.silu(ya)
    for hd in range(A_HEADS):
        cs = slice(hd * A_DK, (hd + 1) * A_DK)
        qh = ya[:, hd * A_DK:(hd + 1) * A_DK]
        kh = ya[:, A_QK + hd * A_DK:A_QK + (hd + 1) * A_DK]
        q_ref[:, cs] = qh * lax.rsqrt(jnp.sum(qh * qh, axis=-1, keepdims=True) + EPS) * (A_DK ** -0.5)
        k_ref[:, cs] = kh * lax.rsqrt(jnp.sum(kh * kh, axis=-1, keepdims=True) + EPS)
    v_ref[...] = ya[:, 2 * A_QK:]

    ba = ba_ref[...]
    lane = lax.broadcasted_iota(jnp.int32, ba.shape, 1)
    beta = jax.nn.sigmoid(ba)
    g = -jnp.exp(alog_ref[...]) * jax.nn.softplus(ba + dtb_ref[...])
    gbeta_ref[...] = jnp.where(lane < A_HEADS, beta, g)


def _even_prep(p, conv_a0, conv_b0, w_conv_a, w_conv_b, a_log, dt_bias, nb, t):
    tt = _row_tile(t, 512)
    nt = t // tt
    n = nb * t
    rows = lambda w, cb: pl.BlockSpec((tt, w), lambda b, j: (b * nt + j, cb))
    pad = lambda a: jnp.zeros((1, LANES), F32).at[0, A_HEADS:2 * A_HEADS].set(a)
    outs = pl.pallas_call(
        functools.partial(_even_prep_kernel, tt=tt),
        grid=(nb, nt),
        in_specs=[rows(A_CONV_DIM, 0), rows(B_WIDTH, EV_H // B_WIDTH), rows(B_WIDTH, EV_GB // B_WIDTH),
                  rows(B_WIDTH, EV_GC // B_WIDTH), rows(LANES, EV_BA // LANES),
                  pl.BlockSpec((None, A_CONV - 1, A_CONV_DIM), lambda b, j: (b, 0, 0)),
                  pl.BlockSpec((None, B_CONV - 1, B_WIDTH), lambda b, j: (b, 0, 0)),
                  _const_spec((A_CONV, A_CONV_DIM)), _const_spec((B_CONV, B_WIDTH)),
                  _const_spec((1, LANES)), _const_spec((1, LANES))],
        out_specs=[rows(A_QK, 0), rows(A_QK, 0), rows(A_V, 0), rows(LANES, 0), rows(B_WIDTH, 0),
                   pl.BlockSpec((None, A_CONV - 1, A_CONV_DIM), lambda b, j: (b, 0, 0)),
                   pl.BlockSpec((None, B_CONV - 1, B_WIDTH), lambda b, j: (b, 0, 0))],
        out_shape=[jax.ShapeDtypeStruct((n, A_QK), F32), jax.ShapeDtypeStruct((n, A_QK), F32),
                   jax.ShapeDtypeStruct((n, A_V), F32), jax.ShapeDtypeStruct((n, LANES), F32),
                   jax.ShapeDtypeStruct((n, B_WIDTH), F32),
                   jax.ShapeDtypeStruct((nb, A_CONV - 1, A_CONV_DIM), F32),
                   jax.ShapeDtypeStruct((nb, B_CONV - 1, B_WIDTH), F32)],
        scratch_shapes=[pltpu.VMEM((tt + CARRY, A_CONV_DIM), F32), pltpu.VMEM((tt + CARRY, B_WIDTH), F32)],
        compiler_params=_cparams(("parallel", "arbitrary")),
        name="even_prep",
    )(p, p, p, p, p, conv_a0, conv_b0, w_conv_a, w_conv_b, pad(a_log), pad(dt_bias))
    return outs


def _cumsum_rows(x):
    n = x.shape[0]
    row = lax.broadcasted_iota(jnp.int32, x.shape, 0)
    s = 1
    while s < n:
        x = x + jnp.where(row >= s, pltpu.roll(x, s, axis=0), 0.0)
        s *= 2
    return x


def _unit_lower_inverse(a, eye):
    c = a.shape[0]
    x = -a
    t = eye + x
    p = x
    s = 2
    while s < c:
        p = _mm(p, p)
        t = t + _mm(t, p)
        s *= 2
    return t


def _gdn_chunk_kernel(q_ref, k_ref, v_ref, gb_ref, z_ref, gn_ref, s0_ref, oa_ref, s_ref, *, c, n_chunks):
    j = pl.program_id(1)

    @pl.when(j == 0)
    def _():
        s_ref[...] = s0_ref[...]

    row = lax.broadcasted_iota(jnp.int32, (c, c), 0)
    col = lax.broadcasted_iota(jnp.int32, (c, c), 1)
    tri_incl = col <= row
    tri_strict = col < row
    eye_b = col == row
    eye = jnp.where(eye_b, 1.0, 0.0)
    gn = gn_ref[...]
    for ci in range(n_chunks):
        rs = slice(ci * c, (ci + 1) * c)
        gbv = gb_ref[rs, :]
        gcs = _cumsum_rows(gbv)
        for hd in range(A_HEADS):
            cs = slice(hd * A_DK, (hd + 1) * A_DK)
            q = q_ref[rs, cs]
            k = k_ref[rs, cs]
            v = v_ref[rs, cs]
            beta = gbv[:, hd:hd + 1]
            gcol = gcs[:, A_HEADS + hd:A_HEADS + hd + 1]
            grow = jnp.sum(jnp.where(eye_b, gcol, 0.0), axis=0, keepdims=True)
            dmask = jnp.where(tri_incl, jnp.exp(jnp.where(tri_incl, gcol - grow, 0.0)), 0.0)
            kb = k * beta
            vb = v * beta
            a_mat = jnp.where(tri_strict, _mm_nt(kb, k) * dmask, 0.0)
            t_mat = _unit_lower_inverse(a_mat, eye)
            egc = jnp.exp(gcol)
            u = _mm(t_mat, vb)
            w = _mm(t_mat, kb * egc)
            intra = jnp.where(tri_incl, _mm_nt(q, k) * dmask, 0.0)
            glast = gcol[c - 1:c, :]
            ktail = k * jnp.exp(glast - gcol)
            qdec = q * egc
            s = s_ref[hd]
            vnew = u - _mm(w, s)
            o = _mm(qdec, s) + _mm(intra, vnew)
            s_ref[hd] = s * jnp.exp(glast) + _mm_tn(ktail, vnew)
            oa_ref[rs, cs] = _rms(o, gn) * jax.nn.silu(z_ref[rs, cs])


def _gdn_chunks(q, k, v, gbeta, p, g_norm, s0, nb, t):
    c = A_CHUNK if t >= A_CHUNK else t
    tc = _row_tile(t, 4 * c)
    nt = t // tc
    n = nb * t
    rows = lambda w, cb: pl.BlockSpec((tc, w), lambda b, j: (b * nt + j, cb))
    st = pl.BlockSpec((None, A_HEADS, A_DK, A_DV), lambda b, j: (b, 0, 0, 0))
    return pl.pallas_call(
        functools.partial(_gdn_chunk_kernel, c=c, n_chunks=tc // c),
        grid=(nb, nt),
        in_specs=[rows(A_QK, 0), rows(A_QK, 0), rows(A_V, 0), rows(LANES, 0), rows(A_V, EV_Z // A_V),
                  _const_spec((1, A_DV)), st],
        out_specs=[rows(A_V, 0), st],
        out_shape=[jax.ShapeDtypeStruct((n, A_V), F32), jax.ShapeDtypeStruct((nb, A_HEADS, A_DK, A_DV), F32)],
        compiler_params=_cparams(("parallel", "arbitrary")),
        name="gdn_chunks",
    )(q, k, v, gbeta, p, g_norm.reshape(1, -1), s0)


def _even_mix(x, g_pre, w_in, s0, conv_a0, conv_b0, a_conv, a_log, dt_bias, a_norm, b_conv, nb, t):
    p = _in_proj(x, g_pre, w_in)
    q, k, v, gbeta, ob, conv_a, conv_b = _even_prep(p, conv_a0, conv_b0, a_conv, b_conv, a_log, dt_bias, nb, t)
    oa, s = _gdn_chunks(q, k, v, gbeta, p, a_norm, s0, nb, t)
    return oa, ob, s, conv_a, conv_b


OD_Q = 0
OD_U = C_Q
OD_V = OD_U + D_WIDTH
OD_KV = OD_V + D_WIDTH
OD_G = OD_KV + C_KV
OD_COLS = OD_G + C_KV_HEADS * LANES
KV_ROW = C_KV // 3
QB = 128
KT = 128


def _odd_w_in(w):
    d = w.shape[0]
    gw = w[:, C_Q:C_Q + 3 * C_HEADS].reshape(d, C_KV_HEADS, C_REP, 3)
    gw = jnp.transpose(gw, (0, 1, 3, 2)).reshape(d, C_KV_HEADS, 3 * C_REP)
    gw = jnp.pad(gw, ((0, 0), (0, 0), (0, LANES - 3 * C_REP))).reshape(d, C_KV_HEADS * LANES)
    kv0 = C_Q + 3 * C_HEADS
    return jnp.concatenate([w[:, :C_Q], w[:, kv0 + C_KV:], w[:, kv0:kv0 + C_KV], gw], axis=1)


def _compress_weights(pe, w1, w2):
    s = C_CMP_STRIDE
    flat = lambda a: a.reshape(2, s * C_HD, C_CMP_HID).astype(BF16)
    dense = (flat(w1[:, :s]), flat(w1[:, s:]), pe[:, :s].reshape(2, 1, s * C_HD), pe[:, s:].reshape(2, 1, s * C_HD),
             w2.astype(BF16))
    def rows(w1h, peh):
        wx = jnp.zeros((2, s, C_KV_HEADS * C_HD, C_KV_HEADS * C_CMP_HID), F32)
        for hd in range(C_KV_HEADS):
            wx = wx.at[:, :, hd * C_HD:(hd + 1) * C_HD, hd * C_CMP_HID:(hd + 1) * C_CMP_HID].set(w1h)
        px = jnp.concatenate([peh] * C_KV_HEADS, axis=-1)
        return wx.astype(BF16), px.reshape(2, s, 1, C_KV_HEADS * C_HD)

    wxa, pxa = rows(w1[:, :s], pe[:, :s])
    wxb, pxb = rows(w1[:, s:], pe[:, s:])
    return dense, (wxa, wxb, pxa, pxb, w2.astype(BF16))


def _cmp_to_sel(nc, ns):
    cs = C_CMP_STRIDE * jnp.arange(nc)[:, None]
    ss = C_SEL_LEN * jnp.arange(ns)[None, :]
    ov = jnp.clip(jnp.minimum(cs + C_CMP_LEN, ss + C_SEL_LEN) - jnp.maximum(cs, ss), 0, None)
    return ov.astype(F32) / C_CMP_LEN


def _block_expand(n_keys):
    return (jnp.arange(n_keys)[None, :] // C_SEL_LEN == jnp.arange(LANES)[:, None]).astype(BF16)


def _chunk_mlp_kernel(u_ref, v_ref, w_ref, bias_ref, lng_ref, lnb_ref, o_ref, vrow_ref, *, span):
    u = jax.nn.gelu(u_ref[...])
    v = jax.nn.gelu(v_ref[...])
    mu = jnp.mean(v, axis=-1, keepdims=True)
    var = jnp.mean(jnp.square(v - mu), axis=-1, keepdims=True)
    vn = (v - mu) * lax.rsqrt(var + EPS) * lng_ref[...] + lnb_ref[...]
    vrow_ref[...] = vn
    n = u.shape[0]
    row = lax.broadcasted_iota(jnp.int32, (n, n), 0)
    col = lax.broadcasted_iota(jnp.int32, (n, n), 1)
    keep = (row - col).astype(jnp.uint32) <= jnp.bitwise_and(row, span - 1).astype(jnp.uint32)
    lane = lax.broadcasted_iota(jnp.int32, (n, LANES), 1)
    gw = D_WIDTH // D_GROUPS
    for pr in range(D_GROUPS // 2):
        cs = slice(pr * LANES, (pr + 1) * LANES)
        vp = vn[:, cs]
        m0 = _mm(jnp.where(keep, w_ref[2 * pr], 0.0), vp)
        m1 = _mm(jnp.where(keep, w_ref[2 * pr + 1], 0.0), vp)
        o_ref[:, cs] = u[:, cs] * (jnp.where(lane < gw, m0, m1) + bias_ref[:, cs])


def _chunk_mlp(p, ws, bs, ln_g, ln_b, n, span):
    tile = D_CHUNK
    reps = tile // span
    wt = jnp.tile(ws[:, :span, :span], (1, reps, reps))
    bt = jnp.repeat(jnp.tile(bs[:, :span].T, (reps, 1)), D_WIDTH // D_GROUPS, axis=1)
    rows = lambda cb: pl.BlockSpec((tile, D_WIDTH), lambda i: (i, cb))
    return pl.pallas_call(
        functools.partial(_chunk_mlp_kernel, span=span),
        grid=(n // tile,),
        in_specs=[rows(OD_U // D_WIDTH), rows(OD_V // D_WIDTH), _const_spec((D_GROUPS, tile, tile)),
                  _const_spec((tile, D_WIDTH)), _const_spec((1, D_WIDTH)), _const_spec((1, D_WIDTH))],
        out_specs=[rows(0), rows(0)],
        out_shape=[jax.ShapeDtypeStruct((n, D_WIDTH), F32), jax.ShapeDtypeStruct((n, D_WIDTH), F32)],
        compiler_params=_cparams(("parallel",)),
        name="chunk_mlp",
    )(p, p, wt, bt, ln_g.reshape(1, -1), ln_b.reshape(1, -1))


def _compress_kernel(x_ref, w1a_ref, w1b_ref, pea_ref, peb_ref, w2_ref, o_ref):
    x = x_ref[...]
    nh = x.shape[0]
    pa = _mm(x + pea_ref[...], w1a_ref[...])
    pb = _mm(x + peb_ref[...], w1b_ref[...])
    hid = jax.nn.silu(pa + pltpu.roll(pb, nh - 1, axis=0))
    o_ref[...] = _mm(hid, w2_ref[...]).astype(o_ref.dtype)


def _compress(xc, cw):
    w1a, w1b, pea, peb, w2 = cw
    nb, n_slab, nh, width = xc.shape
    kvw = lambda shape: pl.BlockSpec((None,) + shape, lambda b, s: (s // C_KV_HEADS, 0, 0))
    return pl.pallas_call(
        _compress_kernel,
        grid=(nb, n_slab),
        in_specs=[pl.BlockSpec((None, None, nh, width), lambda b, s: (b, s, 0, 0)),
                  kvw((width, C_CMP_HID)), kvw((width, C_CMP_HID)), kvw((1, width)), kvw((1, width)),
                  kvw((C_CMP_HID, C_HD))],
        out_specs=pl.BlockSpec((None, None, nh, C_HD), lambda b, s: (b, s, 0, 0)),
        out_shape=jax.ShapeDtypeStruct((nb, n_slab, nh, C_HD), BF16),
        compiler_params=_cparams(("parallel", "parallel")),
        name="compress",
    )(xc, w1a, w1b, pea, peb, w2)


def _msoftmax(s, mask):
    s = jnp.where(mask, s, NEG)
    m = jnp.max(s, axis=-1, keepdims=True)
    e = jnp.where(mask, jnp.exp(s - m), 0.0)
    return e / jnp.maximum(jnp.sum(e, axis=-1, keepdims=True), 1e-30)


def _block_scores(imp, qpos):
    blk = lax.broadcasted_iota(jnp.int32, imp.shape, 1)
    cur = jnp.right_shift(qpos, 6)
    forced = (blk == 0) | (blk == cur) | (blk == cur - 1)
    return jnp.where(blk <= cur, jnp.where(forced, BIG, imp), NEG)


def _topn_mask(st_ref, nblk):
    st = st_ref[0:nblk, :]
    rowi = lax.broadcasted_iota(jnp.int32, st.shape, 0)
    rank = jnp.zeros(st.shape, F32)
    for s2 in range(nblk):
        b = jnp.broadcast_to(st_ref[s2:s2 + 1, :], st.shape)
        rank = rank + jnp.where(b > st, 1.0, jnp.where(b == st, jnp.where(rowi > s2, 1.0, 0.0), 0.0))
    sel_t = jnp.where(rank < C_TOPN, jnp.where(st > 0.5 * NEG, 1.0, 0.0), 0.0)
    if nblk < LANES:
        sel_t = jnp.concatenate([sel_t, jnp.zeros((LANES - nblk, st.shape[1]), F32)], axis=0)
    return sel_t.T


def _nsa_prompt_kernel(q_ref, gate_ref, kc_ref, vc_ref, ks_ref, vs_ref, kw_ref, vw_ref, mcs_ref, e_ref,
                       o_ref, st_ref, mx_ref, m_ref, l_ref, acc_ref, *, nh, nblk):
    g = pl.program_id(1)
    i = pl.program_id(2)
    q0 = i * QB
    qpos = q0 + lax.broadcasted_iota(jnp.int32, (QB, 1), 0)
    slopes = [jnp.where(g == 0, 2.0 ** -(r + 1), 2.0 ** -(r + 1 + C_REP)) for r in range(C_REP)]
    qs = [q_ref[:, r * C_HD:(r + 1) * C_HD] * (C_HD ** -0.5) for r in range(C_REP)]
    gates = jax.nn.sigmoid(gate_ref[...])

    cend = C_CMP_STRIDE * lax.broadcasted_iota(jnp.int32, (1, nh), 1) + (C_CMP_LEN - 1)
    maskc = cend <= qpos
    distc = (qpos - cend).astype(F32)
    kc = kc_ref[...]
    vc = vc_ref[...]
    o_c, ps = [], []
    for r in range(C_REP):
        p = _msoftmax(_mm_nt(qs[r], kc) - slopes[r] * distc, maskc)
        o_c.append(_mm(p, vc))
        ps.append(p.astype(BF16))
    imp = jnp.dot(jnp.concatenate(ps, axis=1), mcs_ref[...], preferred_element_type=F32)

    st_ref[...] = _block_scores(imp, qpos).T
    sel = _topn_mask(st_ref, nblk)
    mx_ref[...] = jnp.dot(sel.astype(BF16), e_ref[...], preferred_element_type=F32)

    def attend(k_ref, v_ref, lo, hi, mask_fn):
        m_ref[...] = jnp.full(m_ref.shape, NEG, F32)
        l_ref[...] = jnp.zeros(l_ref.shape, F32)
        acc_ref[...] = jnp.zeros(acc_ref.shape, F32)

        def body(kt, carry):
            k0 = pl.multiple_of(kt * KT, KT)
            kk = k_ref[pl.ds(k0, KT), :]
            vv = v_ref[pl.ds(k0, KT), :]
            kpos = k0 + lax.broadcasted_iota(jnp.int32, (1, KT), 1)
            dist = qpos - kpos
            msk = mask_fn(k0, dist)
            distf = dist.astype(F32)
            for r in range(C_REP):
                s = jnp.where(msk, _mm_nt(qs[r], kk) - slopes[r] * distf, NEG)
                m_old = m_ref[r]
                m_new = jnp.maximum(m_old, jnp.max(s, axis=-1, keepdims=True))
                p = jnp.where(msk, jnp.exp(s - m_new), 0.0)
                alpha = jnp.exp(m_old - m_new)
                l_ref[r] = alpha * l_ref[r] + jnp.sum(p, axis=-1, keepdims=True)
                acc_ref[r] = alpha * acc_ref[r] + _mm(p, vv)
                m_ref[r] = m_new
            return carry

        lax.fori_loop(lo, hi, body, 0)
        return [acc_ref[r] / jnp.maximum(l_ref[r], 1e-30) for r in range(C_REP)]

    o_s = attend(ks_ref, vs_ref, 0, i + 1,
                 lambda k0, dist: jnp.where(dist >= 0, mx_ref[:, pl.ds(k0, KT)], 0.0) > 0.5)
    o_w = attend(kw_ref, vw_ref, jnp.maximum(i - C_WINDOW // KT, 0), i + 1,
                 lambda k0, dist: dist.astype(jnp.uint32) < C_WINDOW)

    outs = []
    for r in range(C_REP):
        outs.append(gates[:, r:r + 1] * o_c[r] + gates[:, C_REP + r:C_REP + r + 1] * o_s[r]
                    + gates[:, 2 * C_REP + r:2 * C_REP + r + 1] * o_w[r])
    o_ref[...] = jnp.concatenate(outs, axis=1)


def _nsa_prompt(p, cmp, kvs, nb, t):
    nqb = t // QB
    nh = t // C_CMP_STRIDE
    ns = t // C_SEL_LEN
    nblk = -(-ns // 8) * 8
    mcs = jnp.tile(_cmp_to_sel(nh, LANES), (C_REP, 1)).astype(BF16)
    e = _block_expand(t)
    gw = C_REP * C_HD
    slab = lambda rows, off: pl.BlockSpec((None, None, rows, C_HD), lambda b, g, i: (b, off + g, 0, 0))
    return pl.pallas_call(
        functools.partial(_nsa_prompt_kernel, nh=nh, nblk=nblk),
        grid=(nb, C_KV_HEADS, nqb),
        in_specs=[pl.BlockSpec((QB, gw), lambda b, g, i: (b * nqb + i, g)),
                  pl.BlockSpec((QB, LANES), lambda b, g, i: (b * nqb + i, OD_G // LANES + g)),
                  slab(nh, 0), slab(nh, C_KV_HEADS),
                  slab(t, 4), slab(t, 6), slab(t, 8), slab(t, 10),
                  _const_spec((C_REP * nh, LANES)), _const_spec((LANES, t))],
        out_specs=pl.BlockSpec((QB, gw), lambda b, g, i: (b * nqb + i, g)),
        out_shape=jax.ShapeDtypeStruct((nb * t, C_Q), F32),
        scratch_shapes=[pltpu.VMEM((LANES, QB), F32), pltpu.VMEM((QB, t), F32),
                        pltpu.VMEM((C_REP, QB, 1), F32), pltpu.VMEM((C_REP, QB, 1), F32),
                        pltpu.VMEM((C_REP, QB, C_HD), F32)],
        compiler_params=_cparams(("parallel", "parallel", "arbitrary")),
        name="nsa_prompt",
    )(p, p, cmp, cmp, kvs, kvs, kvs, kvs, mcs, e)


def _odd_mix_prompt(x, g_pre, w_in, cw, dw, nb, t):
    dense, _ = cw
    p = _in_proj(x, g_pre, w_in)
    od, dv = _chunk_mlp(p, *dw, nb * t, min(t, D_CHUNK))
    kv_new = p[:, OD_KV:OD_KV + C_KV]
    kvs = jnp.transpose(kv_new.reshape(nb, t, C_KV // C_HD, C_HD), (0, 2, 1, 3)).astype(BF16)
    nh = t // C_CMP_STRIDE
    xc = kv_new[:, :KV_ROW].reshape(nb, nh, C_CMP_STRIDE, KV_ROW // C_HD, C_HD)
    xc = jnp.transpose(xc, (0, 3, 1, 2, 4)).reshape(nb, KV_ROW // C_HD, nh, C_CMP_STRIDE * C_HD)
    cmp = _compress(xc, dense)
    oc = _nsa_prompt(p, cmp, kvs, nb, t)
    return oc, od, kv_new, dv


def _nsa_sample_kernel(pt_ref, q_ref, gate_ref, new_ref, win_ref, wxa_ref, wxb_ref, pxa_ref, pxb_ref, w2_ref,
                       mcs_ref, e_ref, *rest, n_pages, page, ts, q_start, nblk):
    del pt_ref
    cp = (rest[:n_pages], rest[n_pages:2 * n_pages])
    sp = rest[2 * n_pages:3 * n_pages]
    o_ref, st_ref = rest[3 * n_pages:]
    nh = n_pages * page // C_CMP_STRIDE
    per_page = page // C_CMP_STRIDE
    n_win = win_ref.shape[0]
    rows = C_REP * ts

    cmpv = []
    for kv in range(2):
        pa = jnp.zeros((nh, C_KV_HEADS * C_CMP_HID), F32)
        pb = jnp.zeros((nh, C_KV_HEADS * C_CMP_HID), F32)
        for s in range(C_CMP_STRIDE):
            xs = jnp.concatenate([cp[kv][pg][pl.ds(s, per_page, stride=C_CMP_STRIDE), :] for pg in range(n_pages)],
                                 axis=0)
            pa = pa + _mm(xs + pxa_ref[kv, s], wxa_ref[kv, s])
            pb = pb + _mm(xs + pxb_ref[kv, s], wxb_ref[kv, s])
        hid = jax.nn.silu(pa + pltpu.roll(pb, nh - 1, axis=0))
        cmpv += [_mm(hid[:, hd * C_CMP_HID:(hd + 1) * C_CMP_HID], w2_ref[kv]) for hd in range(C_KV_HEADS)]

    qpos_t = q_start + lax.broadcasted_iota(jnp.int32, (ts, 1), 0)
    qpos = jnp.concatenate([qpos_t] * C_REP, axis=0)
    cend = C_CMP_STRIDE * lax.broadcasted_iota(jnp.int32, (1, nh), 1) + (C_CMP_LEN - 1)
    maskc = cend <= qpos
    distc = (qpos - cend).astype(F32)
    zq = jnp.zeros((rows, C_HD), F32)

    qg, slope, o_c, scores = [], [], [], []
    for g in range(C_KV_HEADS):
        heads = [g * C_REP + r for r in range(C_REP)]
        qg.append(jnp.concatenate([q_ref[:, h * C_HD:(h + 1) * C_HD] for h in heads], axis=0) * (C_HD ** -0.5))
        slope.append(jnp.concatenate([jnp.full((ts, 1), 2.0 ** -(h + 1), F32) for h in heads], axis=0))
        p = _msoftmax(_mm_nt(qg[g], cmpv[g]) - slope[g] * distc, maskc)
        o_c.append(_mm(p, cmpv[C_KV_HEADS + g]))
        pm = jnp.dot(p.astype(BF16), mcs_ref[...], preferred_element_type=F32)
        imp = pm[0:ts]
        for r in range(1, C_REP):
            imp = imp + pm[r * ts:(r + 1) * ts]
        scores.append(_block_scores(imp, qpos_t))
    pad = jnp.full((QB - C_KV_HEADS * ts, LANES), NEG, F32)
    st_ref[...] = jnp.concatenate(scores + [pad], axis=0).T
    sel = _topn_mask(st_ref, nblk)

    def softmax_pv(tiles, lanes):
        m = tiles[0][0].max(axis=-1, keepdims=True)
        for s, _, _ in tiles[1:]:
            m = jnp.maximum(m, s.max(axis=-1, keepdims=True))
        l = jnp.zeros((rows, 1), F32)
        acc = jnp.zeros((rows, KV_ROW), F32)
        for s, msk, kv in tiles:
            e = jnp.where(msk, jnp.exp(s - m), 0.0)
            l = l + jnp.sum(e, axis=-1, keepdims=True)
            acc = acc + _mm(e, kv)
        return acc[:, lanes] / jnp.maximum(l, 1e-30)

    kpos_new = q_start + lax.broadcasted_iota(jnp.int32, (1, ts), 1)
    outs = []
    for g in range(C_KV_HEADS):
        qz = jnp.concatenate([qg[g] if sl == g else zq for sl in range(2 * C_KV_HEADS)], axis=1)
        vl = slice((C_KV_HEADS + g) * C_HD, (C_KV_HEADS + g + 1) * C_HD)
        selg = jnp.concatenate([sel[g * ts:(g + 1) * ts]] * C_REP, axis=0)
        mx = jnp.dot(selg.astype(BF16), e_ref[...], preferred_element_type=F32)
        tiles = []
        for pg in range(n_pages):
            kv = sp[pg][...]
            kpos = pg * page + lax.broadcasted_iota(jnp.int32, (1, page), 1)
            dist = qpos - kpos
            msk = jnp.where(dist >= 0, mx[:, pg * page:(pg + 1) * page], 0.0) > 0.5
            tiles.append((jnp.where(msk, _mm_nt(qz, kv) - slope[g] * dist.astype(F32), NEG), msk, kv))
        kv = new_ref[:, KV_ROW:2 * KV_ROW]
        dist = qpos - kpos_new
        msk = jnp.where(dist >= 0, mx[:, n_pages * page:n_pages * page + ts], 0.0) > 0.5
        tiles.append((jnp.where(msk, _mm_nt(qz, kv) - slope[g] * dist.astype(F32), NEG), msk, kv))
        o_s = softmax_pv(tiles, vl)

        tiles = []
        for w0 in range(0, n_win, KT):
            kv = win_ref[w0:w0 + KT, :]
            kpos = (q_start - n_win + w0) + lax.broadcasted_iota(jnp.int32, (1, KT), 1)
            dist = qpos - kpos
            msk = dist.astype(jnp.uint32) < C_WINDOW
            tiles.append((jnp.where(msk, _mm_nt(qz, kv) - slope[g] * dist.astype(F32), NEG), msk, kv))
        kv = new_ref[:, 2 * KV_ROW:3 * KV_ROW]
        dist = qpos - kpos_new
        msk = dist.astype(jnp.uint32) < C_WINDOW
        tiles.append((jnp.where(msk, _mm_nt(qz, kv) - slope[g] * dist.astype(F32), NEG), msk, kv))
        o_w = softmax_pv(tiles, vl)

        gates = jax.nn.sigmoid(gate_ref[:, g * LANES:(g + 1) * LANES])
        for r in range(C_REP):
            rs = slice(r * ts, (r + 1) * ts)
            outs.append(gates[:, r:r + 1] * o_c[g][rs] + gates[:, C_REP + r:C_REP + r + 1] * o_s[rs]
                        + gates[:, 2 * C_REP + r:2 * C_REP + r + 1] * o_w[rs])
    o_ref[...] = jnp.concatenate(outs, axis=1)


def _nsa_sample(p, pool_cmp, pool_sel, win_buf, page_table, layer, cwx, nb, ts, past_len):
    wxa, wxb, pxa, pxb, w2 = cwx
    n_pages = page_table.shape[1]
    page = past_len // n_pages
    nh = past_len // C_CMP_STRIDE
    n_win = win_buf.shape[1]
    assert (past_len + ts) // C_CMP_STRIDE == nh and past_len % KT == 0 and n_win % KT == 0 and ts <= KT
    nblk = -(-(-(-(past_len + ts) // C_SEL_LEN)) // 8) * 8
    mcs = _cmp_to_sel(nh, LANES).astype(BF16)
    e = _block_expand(past_len + KT)
    cst = lambda a: pl.BlockSpec(a.shape, lambda b, pt: (0,) * a.ndim)
    half = KV_ROW // 2
    pg_spec = lambda pg: pl.BlockSpec((None, page, KV_ROW), lambda b, pt: (pt[b, pg], layer, 0))
    half_spec = lambda pg, kv: pl.BlockSpec((None, page, half), lambda b, pt: (pt[b, pg], layer, kv))
    page_specs = ([half_spec(pg, 0) for pg in range(n_pages)] + [half_spec(pg, 1) for pg in range(n_pages)]
                  + [pg_spec(pg) for pg in range(n_pages)])
    consts = (wxa, wxb, pxa, pxb, w2, mcs, e)
    grid_spec = pltpu.PrefetchScalarGridSpec(
        num_scalar_prefetch=1,
        grid=(nb,),
        in_specs=[pl.BlockSpec((ts, C_Q), lambda b, pt: (b, 0)),
                  pl.BlockSpec((ts, C_KV_HEADS * LANES), lambda b, pt: (b, OD_G // (C_KV_HEADS * LANES))),
                  pl.BlockSpec((ts, C_KV), lambda b, pt: (b, OD_KV // C_KV)),
                  pl.BlockSpec((None, n_win, KV_ROW), lambda b, pt: (b, 0, 0))]
                 + [cst(a) for a in consts] + page_specs,
        out_specs=pl.BlockSpec((ts, C_Q), lambda b, pt: (b, 0)),
        scratch_shapes=[pltpu.VMEM((LANES, QB), F32)],
    )
    return pl.pallas_call(
        functools.partial(_nsa_sample_kernel, n_pages=n_pages, page=page, ts=ts, q_start=past_len, nblk=nblk),
        grid_spec=grid_spec,
        out_shape=jax.ShapeDtypeStruct((nb * ts, C_Q), F32),
        compiler_params=_cparams(("parallel",)),
        name="nsa_sample",
    )(page_table, p, p, p, win_buf, *consts, *([pool_cmp] * (2 * n_pages)), *([pool_sel] * n_pages))


def _odd_mix_sample(x, g_pre, w_in, cw, dw, pool_cmp, pool_sel, win_buf, page_table, layer, nb, ts, past_len):
    _, rowwise = cw
    p = _in_proj(x, g_pre, w_in)
    od, dv = _chunk_mlp(p, *dw, nb * ts, ts)
    kv_new = p[:, OD_KV:OD_KV + C_KV]
    oc = _nsa_sample(p, pool_cmp, pool_sel, win_buf.reshape(nb, win_buf.shape[1], KV_ROW), page_table, layer,
                     rowwise, nb, ts, past_len)
    return oc, od, kv_new, dv


def kernel(x_prompt, x_sample, state_delta, state_conv_a, state_conv_b, cache_cmp_kv, cache_sel_kv, cache_win_kv, page_table, norm_g, ffn_gate, ffn_up, ffn_down, ev_w_in, ev_w_out, ev_a_conv, ev_a_log, ev_dt_bias, ev_a_norm, ev_b_conv, od_w_in, od_w_out, od_cmp_pe, od_cmp_w1, od_cmp_w2, od_d_ws, od_d_bs, od_d_ln_g, od_d_ln_b):
    depth = norm_g.shape[0]
    n_even = (depth + 1) // 2
    n_odd = depth // 2
    bf = lambda a: a.astype(BF16)
    bp, tp, _ = x_prompt.shape
    bs, ts, _ = x_sample.shape
    page = cache_cmp_kv.shape[2]
    n_pages = page_table.shape[1]
    past_len = n_pages * page
    n_pool = cache_cmp_kv.shape[0]
    pool_cmp = cache_cmp_kv.reshape(n_pool, n_odd * page, C_KV // 3)
    pool_sel = cache_sel_kv.reshape(n_pool, n_odd * page, C_KV // 3)

    xp = x_prompt.reshape(bp * tp, D_MODEL)
    xs = x_sample.reshape(bs * ts, D_MODEL)
    zeros = lambda *s: jnp.zeros(s, F32)
    res = {k: [] for k in ("dp", "ds", "cap", "cas", "cbp", "cbs", "cmpp", "cmps", "selp", "sels", "winp", "wins", "dvs")}
    for li in range(depth):
        j = li // 2
        g = norm_g[li]
        ffn1 = (bf(ffn_gate[li, 0]), bf(ffn_up[li, 0]), bf(ffn_down[li, 0]))
        ffn2 = (bf(ffn_gate[li, 1]), bf(ffn_up[li, 1]), bf(ffn_down[li, 1]))
        xp = _ffn_half(xp, g[0], g[1], *ffn1)
        xs = _ffn_half(xs, g[0], g[1], *ffn1)
        if li % 2 == 0:
            w_in = bf(_even_w_in(ev_w_in[j]))
            w_out = bf(ev_w_out[j])
            mixw = (ev_a_conv[j], ev_a_log[j], ev_dt_bias[j], ev_a_norm[j], ev_b_conv[j])
            oa, ob, s, ca, cb = _even_mix(xp, g[2], w_in, zeros(bp, A_HEADS, A_DK, A_DV), zeros(bp, A_CONV - 1, A_CONV_DIM),
                                          zeros(bp, B_CONV - 1, B_WIDTH), *mixw, bp, tp)
            xp = _out_proj(xp, oa, ob, g[3], w_out)
            res["dp"].append(s); res["cap"].append(ca); res["cbp"].append(cb)
            oa, ob, s, ca, cb = _even_mix(xs, g[2], w_in, state_delta[:, j], state_conv_a[:, j], state_conv_b[:, j],
                                          *mixw, bs, ts)
            xs = _out_proj(xs, oa, ob, g[3], w_out)
            res["ds"].append(s); res["cas"].append(ca); res["cbs"].append(cb)
        else:
            w_in = bf(_odd_w_in(od_w_in[j]))
            w_out = bf(od_w_out[j])
            cw = _compress_weights(od_cmp_pe[j], od_cmp_w1[j], od_cmp_w2[j])
            dw = (od_d_ws[j], od_d_bs[j], od_d_ln_g[j], od_d_ln_b[j])
            oc, od, kv_new, _ = _odd_mix_prompt(xp, g[2], w_in, cw, dw, bp, tp)
            xp = _out_proj(xp, oc, od, g[3], w_out)
            kv_new = kv_new.reshape(bp, tp, 3, 2, C_KV_HEADS, C_HD)
            res["cmpp"].append(kv_new[:, :, 0]); res["selp"].append(kv_new[:, :, 1])
            res["winp"].append(kv_new[:, tp - min(C_WINDOW, tp):, 2])
            oc, od, kv_new, dv = _odd_mix_sample(xs, g[2], w_in, cw, dw, pool_cmp, pool_sel, cache_win_kv[:, j],
                                                 page_table, j, bs, ts, past_len)
            xs = _out_proj(xs, oc, od, g[3], w_out)
            kv_new = kv_new.reshape(bs, ts, 3, 2, C_KV_HEADS, C_HD)
            res["cmps"].append(kv_new[:, :, 0]); res["sels"].append(kv_new[:, :, 1])
            win_all = jnp.concatenate([cache_win_kv[:, j], kv_new[:, :, 2]], axis=1)
            res["wins"].append(win_all[:, win_all.shape[1] - min(C_WINDOW, past_len + ts):])
            res["dvs"].append(dv.reshape(bs, ts, D_WIDTH))
        xp = _ffn_half(xp, g[4], g[5], *ffn2)
        xs = _ffn_half(xs, g[4], g[5], *ffn2)
    st = lambda k: jnp.stack(res[k], axis=1)
    return (xp.reshape(bp, tp, D_MODEL), xs.reshape(bs, ts, D_MODEL), st("dp"), st("ds"), st("cap"), st("cas"),
            st("cbp"), st("cbs"), st("cmpp"), st("cmps"), st("selp"), st("sels"), st("winp"), st("wins"), st("dvs"))
```

```python
import functools

import jax
import jax.numpy as jnp
from jax import lax
from jax.experimental import pallas as pl
from jax.experimental.pallas import tpu as pltpu

F32 = jnp.float32
BF16 = jnp.bfloat16

D_MODEL = 1024
D_FF = 2816
EPS = 1e-6
A_HEADS = 4
A_DK = 128
A_DV = 128
A_CONV = 4
A_CHUNK = 64
A_QK = A_HEADS * A_DK
A_V = A_HEADS * A_DV
A_CONV_DIM = 2 * A_QK + A_V
B_WIDTH = 512
B_CONV = 3
C_HEADS = 8
C_KV_HEADS = 2
C_REP = C_HEADS // C_KV_HEADS
C_HD = 64
C_CMP_STRIDE = 16
C_CMP_LEN = 32
C_CMP_HID = 128
C_SEL_LEN = 64
C_TOPN = 16
C_WINDOW = 512
C_Q = C_HEADS * C_HD
C_KV = 3 * 2 * C_KV_HEADS * C_HD
D_GROUPS = 8
D_WIDTH = 512
D_CHUNK = 128
BIG = 1e4
NEG = -1e30

LANES = 128
VMEM_LIMIT = 56 * 1024 * 1024


def _cparams(sem):
    return pltpu.CompilerParams(dimension_semantics=sem, vmem_limit_bytes=VMEM_LIMIT)


def _rms(x, g):
    return x * lax.rsqrt(jnp.mean(x * x, axis=-1, keepdims=True) + EPS) * g


def _mm(a, b):
    return jnp.dot(a.astype(BF16), b.astype(BF16), preferred_element_type=F32)


def _mm_nt(a, b):
    return lax.dot_general(a.astype(BF16), b.astype(BF16), (((1,), (1,)), ((), ())),
                           preferred_element_type=F32)


def _mm_tn(a, b):
    return lax.dot_general(a.astype(BF16), b.astype(BF16), (((0,), (0,)), ((), ())),
                           preferred_element_type=F32)


def _row_tile(n, pref):
    t = min(n, pref)
    while n % t:
        t //= 2
    return t


def _const_spec(shape):
    return pl.BlockSpec(shape, lambda *_: (0,) * len(shape))


def _ffn_kernel(x_ref, gpre_ref, gpost_ref, wg_ref, wu_ref, wd_ref, o_ref, *, ff_chunk):
    x = x_ref[...]
    h = _rms(x, gpre_ref[...]).astype(BF16)
    acc = jnp.zeros(x.shape, F32)
    for c in range(D_FF // ff_chunk):
        sl = slice(c * ff_chunk, (c + 1) * ff_chunk)
        a = jnp.dot(h, wg_ref[:, sl], preferred_element_type=F32)
        u = jnp.dot(h, wu_ref[:, sl], preferred_element_type=F32)
        act = (jax.nn.silu(a) * u).astype(BF16)
        acc = acc + jnp.dot(act, wd_ref[sl, :], preferred_element_type=F32)
    o_ref[...] = x + 0.5 * _rms(acc, gpost_ref[...])


def _ffn_half(x, g_pre, g_post, wg, wu, wd):
    n = x.shape[0]
    tm = _row_tile(n, 512)
    row = pl.BlockSpec((tm, D_MODEL), lambda i: (i, 0))
    return pl.pallas_call(
        functools.partial(_ffn_kernel, ff_chunk=D_FF // 2),
        grid=(n // tm,),
        in_specs=[row, _const_spec((1, D_MODEL)), _const_spec((1, D_MODEL)),
                  _const_spec((D_MODEL, D_FF)), _const_spec((D_MODEL, D_FF)), _const_spec((D_FF, D_MODEL))],
        out_specs=row,
        out_shape=jax.ShapeDtypeStruct((n, D_MODEL), F32),
        compiler_params=_cparams(("parallel",)),
        name="ffn_half",
    )(x, g_pre.reshape(1, -1), g_post.reshape(1, -1), wg, wu, wd)


def _proj_kernel(x_ref, g_ref, w_ref, o_ref):
    h = _rms(x_ref[...], g_ref[...]).astype(BF16)
    o_ref[...] = jnp.dot(h, w_ref[...], preferred_element_type=F32)


def _in_proj(x, g, w):
    n = x.shape[0]
    cols = w.shape[1]
    tm = _row_tile(n, 512)
    return pl.pallas_call(
        _proj_kernel,
        grid=(n // tm,),
        in_specs=[pl.BlockSpec((tm, D_MODEL), lambda i: (i, 0)), _const_spec((1, D_MODEL)),
                  _const_spec((D_MODEL, cols))],
        out_specs=pl.BlockSpec((tm, cols), lambda i: (i, 0)),
        out_shape=jax.ShapeDtypeStruct((n, cols), F32),
        compiler_params=_cparams(("parallel",)),
        name="in_proj",
    )(x, g.reshape(1, -1), w)


def _out_proj_kernel(x_ref, oa_ref, ob_ref, g_ref, w_ref, o_ref):
    half = oa_ref.shape[1]
    y = (jnp.dot(oa_ref[...].astype(BF16), w_ref[:half, :], preferred_element_type=F32)
         + jnp.dot(ob_ref[...].astype(BF16), w_ref[half:, :], preferred_element_type=F32))
    o_ref[...] = x_ref[...] + _rms(y, g_ref[...])


def _out_proj(x, oa, ob, g, w):
    n = x.shape[0]
    tm = _row_tile(n, 512)
    half = oa.shape[1]
    return pl.pallas_call(
        _out_proj_kernel,
        grid=(n // tm,),
        in_specs=[pl.BlockSpec((tm, D_MODEL), lambda i: (i, 0)),
                  pl.BlockSpec((tm, half), lambda i: (i, 0)), pl.BlockSpec((tm, half), lambda i: (i, 0)),
                  _const_spec((1, D_MODEL)), _const_spec((2 * half, D_MODEL))],
        out_specs=pl.BlockSpec((tm, D_MODEL), lambda i: (i, 0)),
        out_shape=jax.ShapeDtypeStruct((n, D_MODEL), F32),
        compiler_params=_cparams(("parallel",)),
        name="out_proj",
    )(x, oa, ob, g.reshape(1, -1), w)


EV_QKV = 0
EV_Z = A_CONV_DIM
EV_H = EV_Z + A_V
EV_GB = EV_H + B_WIDTH
EV_GC = EV_GB + B_WIDTH
EV_BA = EV_GC + B_WIDTH
EV_COLS = EV_BA + LANES
CARRY = 8


def _even_w_in(w):
    a_in = A_CONV_DIM + A_V + 2 * A_HEADS
    pad = jnp.zeros((w.shape[0], LANES - 2 * A_HEADS), w.dtype)
    return jnp.concatenate([w[:, :A_CONV_DIM + A_V], w[:, a_in:], w[:, A_CONV_DIM + A_V:a_in], pad], axis=1)


def _even_prep_kernel(qkv_ref, h_ref, gb_ref, gc_ref, ba_ref, ca0_ref, cb0_ref, wa_ref, wb_ref, alog_ref,
                      dtb_ref, q_ref, k_ref, v_ref, gbeta_ref, ob_ref, can_ref, cbn_ref, xa, xb, *, tt):
    j = pl.program_id(1)

    @pl.when(j == 0)
    def _():
        xa[CARRY - (A_CONV - 1):CARRY, :] = ca0_ref[...]
        xb[CARRY - (B_CONV - 1):CARRY, :] = cb0_ref[...]

    @pl.when(j > 0)
    def _():
        xa[0:CARRY, :] = xa[tt:tt + CARRY, :]
        xb[0:CARRY, :] = xb[tt:tt + CARRY, :]

    xa[CARRY:CARRY + tt, :] = qkv_ref[...]
    xb[CARRY:CARRY + tt, :] = gc_ref[...] * h_ref[...]

    ya = wa_ref[0:1, :] * xa[pl.ds(CARRY - 3, tt), :]
    for i in range(1, A_CONV):
        ya = ya + wa_ref[i:i + 1, :] * xa[pl.ds(CARRY - 3 + i, tt), :]
    yb = wb_ref[0:1, :] * xb[pl.ds(CARRY - 2, tt), :]
    for i in range(1, B_CONV):
        yb = yb + wb_ref[i:i + 1, :] * xb[pl.ds(CARRY - 2 + i, tt), :]

    ob_ref[...] = gb_ref[...] * yb
    can_ref[...] = xa[tt + CARRY - (A_CONV - 1):tt + CARRY, :]
    cbn_ref[...] = xb[tt + CARRY - (B_CONV - 1):tt + CARRY, :]

    ya = jax.nn.silu(ya)
    for hd in range(A_HEADS):
        cs = slice(hd * A_DK, (hd + 1) * A_DK)
        qh = ya[:, hd * A_DK:(hd + 1) * A_DK]
        kh = ya[:, A_QK + hd * A_DK:A_QK + (hd + 1) * A_DK]
        q_ref[:, cs] = qh * lax.rsqrt(jnp.sum(qh * qh, axis=-1, keepdims=True) + EPS) * (A_DK ** -0.5)
        k_ref[:, cs] = kh * lax.rsqrt(jnp.sum(kh * kh, axis=-1, keepdims=True) + EPS)
    v_ref[...] = ya[:, 2 * A_QK:]

    ba = ba_ref[...]
    lane = lax.broadcasted_iota(jnp.int32, ba.shape, 1)
    beta = jax.nn.sigmoid(ba)
    g = -jnp.exp(alog_ref[...]) * jax.nn.softplus(ba + dtb_ref[...])
    gbeta_ref[...] = jnp.where(lane < A_HEADS, beta, g)


def _even_prep(p, conv_a0, conv_b0, w_conv_a, w_conv_b, a_log, dt_bias, nb, t):
    tt = _row_tile(t, 512)
    nt = t // tt
    n = nb * t
    rows = lambda w, cb: pl.BlockSpec((tt, w), lambda b, j: (b * nt + j, cb))
    pad = lambda a: jnp.zeros((1, LANES), F32).at[0, A_HEADS:2 * A_HEADS].set(a)
    outs = pl.pallas_call(
        functools.partial(_even_prep_kernel, tt=tt),
        grid=(nb, nt),
        in_specs=[rows(A_CONV_DIM, 0), rows(B_WIDTH, EV_H // B_WIDTH), rows(B_WIDTH, EV_GB // B_WIDTH),
                  rows(B_WIDTH, EV_GC // B_WIDTH), rows(LANES, EV_BA // LANES),
                  pl.BlockSpec((None, A_CONV - 1, A_CONV_DIM), lambda b, j: (b, 0, 0)),
                  pl.BlockSpec((None, B_CONV - 1, B_WIDTH), lambda b, j: (b, 0, 0)),
                  _const_spec((A_CONV, A_CONV_DIM)), _const_spec((B_CONV, B_WIDTH)),
                  _const_spec((1, LANES)), _const_spec((1, LANES))],
        out_specs=[rows(A_QK, 0), rows(A_QK, 0), rows(A_V, 0), rows(LANES, 0), rows(B_WIDTH, 0),
                   pl.BlockSpec((None, A_CONV - 1, A_CONV_DIM), lambda b, j: (b, 0, 0)),
                   pl.BlockSpec((None, B_CONV - 1, B_WIDTH), lambda b, j: (b, 0, 0))],
        out_shape=[jax.ShapeDtypeStruct((n, A_QK), F32), jax.ShapeDtypeStruct((n, A_QK), F32),
                   jax.ShapeDtypeStruct((n, A_V), F32), jax.ShapeDtypeStruct((n, LANES), F32),
                   jax.ShapeDtypeStruct((n, B_WIDTH), F32),
                   jax.ShapeDtypeStruct((nb, A_CONV - 1, A_CONV_DIM), F32),
                   jax.ShapeDtypeStruct((nb, B_CONV - 1, B_WIDTH), F32)],
        scratch_shapes=[pltpu.VMEM((tt + CARRY, A_CONV_DIM), F32), pltpu.VMEM((tt + CARRY, B_WIDTH), F32)],
        compiler_params=_cparams(("parallel", "arbitrary")),
        name="even_prep",
    )(p, p, p, p, p, conv_a0, conv_b0, w_conv_a, w_conv_b, pad(a_log), pad(dt_bias))
    return outs


def _cumsum_rows(x):
    n = x.shape[0]
    row = lax.broadcasted_iota(jnp.int32, x.shape, 0)
    s = 1
    while s < n:
        x = x + jnp.where(row >= s, pltpu.roll(x, s, axis=0), 0.0)
        s *= 2
    return x


def _unit_lower_inverse(a, eye):
    c = a.shape[0]
    x = -a
    t = eye + x
    p = x
    s = 2
    while s < c:
        p = _mm(p, p)
        t = t + _mm(t, p)
        s *= 2
    return t


def _gdn_chunk_kernel(q_ref, k_ref, v_ref, gb_ref, z_ref, gn_ref, s0_ref, oa_ref, s_ref, *, c, n_chunks):
    j = pl.program_id(1)

    @pl.when(j == 0)
    def _():
        s_ref[...] = s0_ref[...]

    row = lax.broadcasted_iota(jnp.int32, (c, c), 0)
    col = lax.broadcasted_iota(jnp.int32, (c, c), 1)
    tri_incl = col <= row
    tri_strict = col < row
    eye_b = col == row
    eye = jnp.where(eye_b, 1.0, 0.0)
    gn = gn_ref[...]
    for ci in range(n_chunks):
        rs = slice(ci * c, (ci + 1) * c)
        gbv = gb_ref[rs, :]
        gcs = _cumsum_rows(gbv)
        for hd in range(A_HEADS):
            cs = slice(hd * A_DK, (hd + 1) * A_DK)
            q = q_ref[rs, cs]
            k = k_ref[rs, cs]
            v = v_ref[rs, cs]
            beta = gbv[:, hd:hd + 1]
            gcol = gcs[:, A_HEADS + hd:A_HEADS + hd + 1]
            grow = jnp.sum(jnp.where(eye_b, gcol, 0.0), axis=0, keepdims=True)
            dmask = jnp.where(tri_incl, jnp.exp(jnp.where(tri_incl, gcol - grow, 0.0)), 0.0)
            kb = k * beta
            vb = v * beta
            a_mat = jnp.where(tri_strict, _mm_nt(kb, k) * dmask, 0.0)
            t_mat = _unit_lower_inverse(a_mat, eye)
            egc = jnp.exp(gcol)
            u = _mm(t_mat, vb)
            w = _mm(t_mat, kb * egc)
            intra = jnp.where(tri_incl, _mm_nt(q, k) * dmask, 0.0)
            glast = gcol[c - 1:c, :]
            ktail = k * jnp.exp(glast - gcol)
            qdec = q * egc
            s = s_ref[hd]
            vnew = u - _mm(w, s)
            o = _mm(qdec, s) + _mm(intra, vnew)
            s_ref[hd] = s * jnp.exp(glast) + _mm_tn(ktail, vnew)
            oa_ref[rs, cs] = _rms(o, gn) * jax.nn.silu(z_ref[rs, cs])


def _gdn_chunks(q, k, v, gbeta, p, g_norm, s0, nb, t):
    c = A_CHUNK if t >= A_CHUNK else t
    tc = _row_tile(t, 4 * c)
    nt = t // tc
    n = nb * t
    rows = lambda w, cb: pl.BlockSpec((tc, w), lambda b, j: (b * nt + j, cb))
    st = pl.BlockSpec((None, A_HEADS, A_DK, A_DV), lambda b, j: (b, 0, 0, 0))
    return pl.pallas_call(
        functools.partial(_gdn_chunk_kernel, c=c, n_chunks=tc // c),
        grid=(nb, nt),
        in_specs=[rows(A_QK, 0), rows(A_QK, 0), rows(A_V, 0), rows(LANES, 0), rows(A_V, EV_Z // A_V),
                  _const_spec((1, A_DV)), st],
        out_specs=[rows(A_V, 0), st],
        out_shape=[jax.ShapeDtypeStruct((n, A_V), F32), jax.ShapeDtypeStruct((nb, A_HEADS, A_DK, A_DV), F32)],
        compiler_params=_cparams(("parallel", "arbitrary")),
        name="gdn_chunks",
    )(q, k, v, gbeta, p, g_norm.reshape(1, -1), s0)


def _even_mix(x, g_pre, w_in, s0, conv_a0, conv_b0, a_conv, a_log, dt_bias, a_norm, b_conv, nb, t):
    p = _in_proj(x, g_pre, w_in)
    q, k, v, gbeta, ob, conv_a, conv_b = _even_prep(p, conv_a0, conv_b0, a_conv, b_conv, a_log, dt_bias, nb, t)
    oa, s = _gdn_chunks(q, k, v, gbeta, p, a_norm, s0, nb, t)
    return oa, ob, s, conv_a, conv_b


OD_Q = 0
OD_U = C_Q
OD_V = OD_U + D_WIDTH
OD_KV = OD_V + D_WIDTH
OD_G = OD_KV + C_KV
OD_COLS = OD_G + C_KV_HEADS * LANES
KV_ROW = C_KV // 3
QB = 128
KT = 128
SEL_UNROLL = 4
WIN_TILES = C_WINDOW // KT + 1
K_AUG = 128


def _with_positions(k, pos):
    assert int(pos.shape[0]) <= 256 * C_SEL_LEN
    extra = jnp.stack([pos // C_SEL_LEN, pos % C_SEL_LEN, jnp.ones_like(pos), jnp.ones_like(pos)], axis=-1)
    extra = jnp.pad(extra.astype(BF16), ((0, 0), (0, K_AUG - C_HD - 4)))
    return jnp.concatenate([k, jnp.broadcast_to(extra, k.shape[:-1] + extra.shape[-1:])], axis=-1)


def _odd_w_in(w):
    d = w.shape[0]
    gw = w[:, C_Q:C_Q + 3 * C_HEADS].reshape(d, C_KV_HEADS, C_REP, 3)
    gw = jnp.transpose(gw, (0, 1, 3, 2)).reshape(d, C_KV_HEADS, 3 * C_REP)
    gw = jnp.pad(gw, ((0, 0), (0, 0), (0, LANES - 3 * C_REP))).reshape(d, C_KV_HEADS * LANES)
    kv0 = C_Q + 3 * C_HEADS
    return jnp.concatenate([w[:, :C_Q], w[:, kv0 + C_KV:], w[:, kv0:kv0 + C_KV], gw], axis=1)


def _compress_weights(pe, w1, w2):
    s = C_CMP_STRIDE
    flat = lambda a: a.reshape(2, s * C_HD, C_CMP_HID).astype(BF16)
    dense = (flat(w1[:, :s]), flat(w1[:, s:]), pe[:, :s].reshape(2, 1, s * C_HD), pe[:, s:].reshape(2, 1, s * C_HD),
             w2.astype(BF16))
    def rows(w1h, peh):
        wx = jnp.zeros((2, s, C_KV_HEADS * C_HD, C_KV_HEADS * C_CMP_HID), F32)
        for hd in range(C_KV_HEADS):
            wx = wx.at[:, :, hd * C_HD:(hd + 1) * C_HD, hd * C_CMP_HID:(hd + 1) * C_CMP_HID].set(w1h)
        px = jnp.concatenate([peh] * C_KV_HEADS, axis=-1)
        return wx.astype(BF16), px.reshape(2, s, 1, C_KV_HEADS * C_HD)

    wxa, pxa = rows(w1[:, :s], pe[:, :s])
    wxb, pxb = rows(w1[:, s:], pe[:, s:])
    return dense, (wxa, wxb, pxa, pxb, w2.astype(BF16))


def _cmp_to_sel(nc, ns):
    cs = C_CMP_STRIDE * jnp.arange(nc)[:, None]
    ss = C_SEL_LEN * jnp.arange(ns)[None, :]
    ov = jnp.clip(jnp.minimum(cs + C_CMP_LEN, ss + C_SEL_LEN) - jnp.maximum(cs, ss), 0, None)
    return ov.astype(F32) / C_CMP_LEN


def _block_expand(n_keys):
    return (jnp.arange(n_keys)[None, :] // C_SEL_LEN == jnp.arange(LANES)[:, None]).astype(BF16)


def _chunk_mlp_kernel(u_ref, v_ref, w_ref, bias_ref, lng_ref, lnb_ref, o_ref, vrow_ref, *, span):
    u = jax.nn.gelu(u_ref[...])
    v = jax.nn.gelu(v_ref[...])
    mu = jnp.mean(v, axis=-1, keepdims=True)
    var = jnp.mean(jnp.square(v - mu), axis=-1, keepdims=True)
    vn = (v - mu) * lax.rsqrt(var + EPS) * lng_ref[...] + lnb_ref[...]
    vrow_ref[...] = vn
    n = u.shape[0]
    row = lax.broadcasted_iota(jnp.int32, (n, n), 0)
    col = lax.broadcasted_iota(jnp.int32, (n, n), 1)
    keep = (row - col).astype(jnp.uint32) <= jnp.bitwise_and(row, span - 1).astype(jnp.uint32)
    lane = lax.broadcasted_iota(jnp.int32, (n, LANES), 1)
    gw = D_WIDTH // D_GROUPS
    for pr in range(D_GROUPS // 2):
        cs = slice(pr * LANES, (pr + 1) * LANES)
        vp = vn[:, cs]
        m0 = _mm(jnp.where(keep, w_ref[2 * pr], 0.0), vp)
        m1 = _mm(jnp.where(keep, w_ref[2 * pr + 1], 0.0), vp)
        o_ref[:, cs] = u[:, cs] * (jnp.where(lane < gw, m0, m1) + bias_ref[:, cs])


def _chunk_mlp(p, ws, bs, ln_g, ln_b, n, span):
    tile = D_CHUNK
    reps = tile // span
    wt = jnp.tile(ws[:, :span, :span], (1, reps, reps))
    bt = jnp.repeat(jnp.tile(bs[:, :span].T, (reps, 1)), D_WIDTH // D_GROUPS, axis=1)
    rows = lambda cb: pl.BlockSpec((tile, D_WIDTH), lambda i: (i, cb))
    return pl.pallas_call(
        functools.partial(_chunk_mlp_kernel, span=span),
        grid=(n // tile,),
        in_specs=[rows(OD_U // D_WIDTH), rows(OD_V // D_WIDTH), _const_spec((D_GROUPS, tile, tile)),
                  _const_spec((tile, D_WIDTH)), _const_spec((1, D_WIDTH)), _const_spec((1, D_WIDTH))],
        out_specs=[rows(0), rows(0)],
        out_shape=[jax.ShapeDtypeStruct((n, D_WIDTH), F32), jax.ShapeDtypeStruct((n, D_WIDTH), F32)],
        compiler_params=_cparams(("parallel",)),
        name="chunk_mlp",
    )(p, p, wt, bt, ln_g.reshape(1, -1), ln_b.reshape(1, -1))


def _compress_kernel(x_ref, w1a_ref, w1b_ref, pea_ref, peb_ref, w2_ref, o_ref):
    x = x_ref[...]
    nh = x.shape[0]
    pa = _mm(x + pea_ref[...], w1a_ref[...])
    pb = _mm(x + peb_ref[...], w1b_ref[...])
    hid = jax.nn.silu(pa + pltpu.roll(pb, nh - 1, axis=0))
    o_ref[...] = _mm(hid, w2_ref[...]).astype(o_ref.dtype)


def _compress(xc, cw):
    w1a, w1b, pea, peb, w2 = cw
    nb, n_slab, nh, width = xc.shape
    kvw = lambda shape: pl.BlockSpec((None,) + shape, lambda b, s: (s // C_KV_HEADS, 0, 0))
    return pl.pallas_call(
        _compress_kernel,
        grid=(nb, n_slab),
        in_specs=[pl.BlockSpec((None, None, nh, width), lambda b, s: (b, s, 0, 0)),
                  kvw((width, C_CMP_HID)), kvw((width, C_CMP_HID)), kvw((1, width)), kvw((1, width)),
                  kvw((C_CMP_HID, C_HD))],
        out_specs=pl.BlockSpec((None, None, nh, C_HD), lambda b, s: (b, s, 0, 0)),
        out_shape=jax.ShapeDtypeStruct((nb, n_slab, nh, C_HD), BF16),
        compiler_params=_cparams(("parallel", "parallel")),
        name="compress",
    )(xc, w1a, w1b, pea, peb, w2)


def _msoftmax(s, mask):
    s = jnp.where(mask, s, NEG)
    m = jnp.max(s, axis=-1, keepdims=True)
    e = jnp.where(mask, jnp.exp(s - m), 0.0)
    return e / jnp.maximum(jnp.sum(e, axis=-1, keepdims=True), 1e-30)


def _block_scores(imp, qpos):
    blk = lax.broadcasted_iota(jnp.int32, imp.shape, 1)
    cur = jnp.right_shift(qpos, 6)
    forced = (blk == 0) | (blk == cur) | (blk == cur - 1)
    return jnp.where(blk <= cur, jnp.where(forced, BIG, imp), NEG)


def _topn_rows(st_ref, nblk):
    st = st_ref[0:nblk, :]
    rowi = lax.broadcasted_iota(jnp.int32, st.shape, 0)
    rank = jnp.zeros(st.shape, F32)
    for s2 in range(nblk):
        b = jnp.broadcast_to(st_ref[s2:s2 + 1, :], st.shape)
        rank = rank + jnp.where(b > st, 1.0, jnp.where(b == st, jnp.where(rowi > s2, 1.0, 0.0), 0.0))
    return jnp.where(rank < C_TOPN, jnp.where(st > 0.5 * NEG, 1.0, 0.0), 0.0)


def _topn_mask(st_ref, nblk):
    sel_t = _topn_rows(st_ref, nblk)
    if nblk < LANES:
        sel_t = jnp.concatenate([sel_t, jnp.zeros((LANES - nblk, sel_t.shape[1]), F32)], axis=0)
    return sel_t.T


def _nsa_prompt_kernel(q_ref, gate_ref, kc_ref, vct_ref, ks_ref, vst_ref, kw_ref, vwt_ref, mcst_ref, et_ref,
                       o_ref, st_ref, s_ref, m_ref, l_ref, acc_ref, *, nh, nblk):
    g = pl.program_id(1)
    i = pl.program_id(2)
    qpos = i * QB + lax.broadcasted_iota(jnp.int32, (1, QB), 1)
    q_hi = jnp.right_shift(qpos, 6).astype(F32)
    q_lo = jnp.bitwise_and(qpos, C_SEL_LEN - 1).astype(F32)
    lanes = [slice(r * QB, (r + 1) * QB) for r in range(C_REP)]
    qt = q_ref[...].T
    sub = lax.broadcasted_iota(jnp.int32, (8, QB), 0)
    pos_rows = []
    for r in range(C_REP):
        slope = jnp.where(g == 0, 2.0 ** -(r + 1), 2.0 ** -(r + 1 + C_REP))
        pos_rows.append(jnp.where(sub == 0, 64.0 * slope, jnp.where(sub == 1, slope, jnp.where(
            sub == 2, -64.0 * slope * q_hi, jnp.where(sub == 3, -slope * q_lo, 0.0)))))
    q_t = jnp.concatenate([
        jnp.concatenate([qt[r * C_HD:(r + 1) * C_HD] for r in range(C_REP)], axis=1) * (C_HD ** -0.5),
        jnp.concatenate(pos_rows, axis=1),
        jnp.zeros((K_AUG - C_HD - 8, C_REP * QB), F32)], axis=0).astype(BF16)
    gate_t = jax.nn.sigmoid(gate_ref[...]).T

    distc = qpos - (C_CMP_STRIDE * lax.broadcasted_iota(jnp.int32, (nh, 1), 0) + (C_CMP_LEN - 1))
    maskc = distc >= 0
    sc = jnp.dot(kc_ref[...], q_t, preferred_element_type=F32)
    ps = []
    for r in range(C_REP):
        s = jnp.where(maskc, sc[:, lanes[r]], NEG)
        e = jnp.where(maskc, jnp.exp(s - jnp.max(s, axis=0, keepdims=True)), 0.0)
        ps.append((e / jnp.maximum(jnp.sum(e, axis=0, keepdims=True), 1e-30)).astype(BF16))
    o_c = jnp.dot(vct_ref[...], jnp.concatenate(ps, axis=1), preferred_element_type=F32)
    imp = jnp.dot(mcst_ref[...], ps[0], preferred_element_type=F32)
    for r in range(1, C_REP):
        imp = imp + jnp.dot(mcst_ref[...], ps[r], preferred_element_type=F32)

    blk = lax.broadcasted_iota(jnp.int32, (LANES, QB), 0)
    cur = jnp.right_shift(qpos, 6)
    forced = (blk == 0) | (blk == cur) | (blk == cur - 1)
    st_ref[...] = jnp.where(blk <= cur, jnp.where(forced, BIG, imp), NEG)
    sel_t = _topn_rows(st_ref, nblk).astype(BF16)

    key_row = lax.broadcasted_iota(jnp.int32, (KT, 1), 0)

    cat = lambda xs: jnp.concatenate(xs, axis=1)

    def group_update(k_ref, vt_ref, k0s, msks, state):
        m_old, l_old, acc = state
        mx = [m_old[:, lanes[r]] for r in range(C_REP)]
        for u, (k0, msk) in enumerate(zip(k0s, msks)):
            s_all = jnp.dot(k_ref[pl.ds(k0, KT), :], q_t, preferred_element_type=F32)
            for r in range(C_REP):
                s = jnp.where(msk, s_all[:, lanes[r]], NEG)
                s_ref[u * KT:(u + 1) * KT, lanes[r]] = s
                mx[r] = jnp.maximum(mx[r], jnp.max(s, axis=0, keepdims=True))
        m_new = cat(mx)
        alpha = jnp.exp(m_old - m_new)
        lsum = [jnp.zeros((1, QB), F32)] * C_REP
        pv = None
        for u, k0 in enumerate(k0s):
            ps = []
            for r in range(C_REP):
                p = jnp.exp(s_ref[u * KT:(u + 1) * KT, lanes[r]] - mx[r])
                lsum[r] = lsum[r] + jnp.sum(p, axis=0, keepdims=True)
                ps.append(p.astype(BF16))
            d = jnp.dot(vt_ref[:, pl.ds(k0, KT)], cat(ps), preferred_element_type=F32)
            pv = d if pv is None else pv + d
        return m_new, alpha * l_old + cat(lsum), alpha * acc + pv

    empty = (jnp.full((1, C_REP * QB), NEG, F32), jnp.zeros((1, C_REP * QB), F32),
             jnp.zeros((C_HD, C_REP * QB), F32))

    m_ref[...], l_ref[...], acc_ref[...] = empty

    def sel_body(jt, carry):
        k0s, msks = [], []
        for u in range(SEL_UNROLL):
            k0 = pl.multiple_of((jt * SEL_UNROLL + u) * KT, KT)
            chosen = jnp.dot(et_ref[pl.ds(k0, KT), :], sel_t, preferred_element_type=F32)
            k0s.append(k0)
            msks.append(jnp.where(qpos >= k0 + key_row, chosen, 0.0) > 0.5)
        m_ref[...], l_ref[...], acc_ref[...] = group_update(ks_ref, vst_ref, k0s, msks,
                                                            (m_ref[...], l_ref[...], acc_ref[...]))
        return carry

    lax.fori_loop(0, i // SEL_UNROLL + 1, sel_body, 0)
    o_s = acc_ref[...] / jnp.maximum(l_ref[...], 1e-30)

    k0s, msks = [], []
    for u in range(WIN_TILES):
        kt = i - C_WINDOW // KT + u
        kpos = kt * KT + key_row
        k0s.append(pl.multiple_of(jnp.maximum(kt, 0) * KT, KT))
        msks.append(jnp.where(kpos >= 0, qpos - kpos, -1).astype(jnp.uint32) < C_WINDOW)
    _, l_w, acc_w = group_update(kw_ref, vwt_ref, k0s, msks, empty)
    o_w = acc_w / jnp.maximum(l_w, 1e-30)

    outs = []
    for r in range(C_REP):
        outs.append(gate_t[r:r + 1] * o_c[:, lanes[r]] + gate_t[C_REP + r:C_REP + r + 1] * o_s[:, lanes[r]]
                    + gate_t[2 * C_REP + r:2 * C_REP + r + 1] * o_w[:, lanes[r]])
    o_ref[...] = jnp.concatenate(outs, axis=0).T


def _nsa_prompt(p, kc, vct, kk, vt, nb, t):
    assert t % (SEL_UNROLL * KT) == 0 and QB == KT
    nqb = t // QB
    nh = t // C_CMP_STRIDE
    ns = t // C_SEL_LEN
    nblk = -(-ns // 8) * 8
    mcst =_cmp_to_sel(nh, LANES).T.astype(BF16)
    et = _block_expand(t).T[:, :nblk]
    gw = C_REP * C_HD
    kspec = lambda br: pl.BlockSpec((None, None, None, t, K_AUG), lambda b, g, i: (b, br, g, 0, 0))
    vspec = lambda br: pl.BlockSpec((None, None, None, C_HD, t), lambda b, g, i: (b, br, g, 0, 0))
    return pl.pallas_call(
        functools.partial(_nsa_prompt_kernel, nh=nh, nblk=nblk),
        grid=(nb, C_KV_HEADS, nqb),
        in_specs=[pl.BlockSpec((QB, gw), lambda b, g, i: (b * nqb + i, g)),
                  pl.BlockSpec((QB, LANES), lambda b, g, i: (b * nqb + i, OD_G // LANES + g)),
                  pl.BlockSpec((None, None, nh, K_AUG), lambda b, g, i: (b, g, 0, 0)),
                  pl.BlockSpec((None, None, C_HD, nh), lambda b, g, i: (b, g, 0, 0)),
                  kspec(0), vspec(0), kspec(1), vspec(1),
                  _const_spec((LANES, nh)), _const_spec((t, nblk))],
        out_specs=pl.BlockSpec((QB, gw), lambda b, g, i: (b * nqb + i, g)),
        out_shape=jax.ShapeDtypeStruct((nb * t, C_Q), F32),
        scratch_shapes=[pltpu.VMEM((LANES, QB), F32),
                        pltpu.VMEM((max(SEL_UNROLL, WIN_TILES) * KT, C_REP * QB), F32),
                        pltpu.VMEM((1, C_REP * QB), F32), pltpu.VMEM((1, C_REP * QB), F32),
                        pltpu.VMEM((C_HD, C_REP * QB), F32)],
        compiler_params=_cparams(("parallel", "parallel", "arbitrary")),
        name="nsa_prompt",
    )(p, p, kc, vct, kk, vt, kk, vt, mcst, et)


def _odd_mix_prompt(x, g_pre, w_in, cw, dw, nb, t):
    dense, _ = cw
    p = _in_proj(x, g_pre, w_in)
    od, dv = _chunk_mlp(p, *dw, nb * t, min(t, D_CHUNK))
    kv_new = p[:, OD_KV:OD_KV + C_KV]
    kvr = kv_new.reshape(nb, t, 3, 2, C_KV_HEADS, C_HD)
    kk = jnp.transpose(kvr[:, :, 1:, 0], (0, 2, 3, 1, 4)).astype(BF16)
    kk = _with_positions(kk, jnp.arange(t, dtype=jnp.int32))
    vt = jnp.transpose(kvr[:, :, 1:, 1], (0, 2, 3, 4, 1)).astype(BF16)
    nh = t // C_CMP_STRIDE
    xc = kv_new[:, :KV_ROW].reshape(nb, nh, C_CMP_STRIDE, KV_ROW // C_HD, C_HD)
    xc = jnp.transpose(xc, (0, 3, 1, 2, 4)).reshape(nb, KV_ROW // C_HD, nh, C_CMP_STRIDE * C_HD)
    cmp = _compress(xc, dense)
    kc = _with_positions(cmp[:, :C_KV_HEADS], C_CMP_STRIDE * jnp.arange(nh, dtype=jnp.int32) + (C_CMP_LEN - 1))
    vct =jnp.transpose(cmp[:, C_KV_HEADS:], (0, 1, 3, 2))
    oc = _nsa_prompt(p, kc, vct, kk, vt, nb, t)
    return oc, od, kv_new, dv


def _nsa_sample_kernel(pt_ref, q_ref, gate_ref, new_ref, win_ref, wxa_ref, wxb_ref, pxa_ref, pxb_ref, w2_ref,
                       mcs_ref, e_ref, *rest, n_pages, page, ts, q_start, nblk):
    del pt_ref
    cp = (rest[:n_pages], rest[n_pages:2 * n_pages])
    sp = rest[2 * n_pages:3 * n_pages]
    o_ref, st_ref = rest[3 * n_pages:]
    nh = n_pages * page // C_CMP_STRIDE
    per_page = page // C_CMP_STRIDE
    n_win = win_ref.shape[0]
    rows = C_REP * ts

    cmpv = []
    for kv in range(2):
        pa = jnp.zeros((nh, C_KV_HEADS * C_CMP_HID), F32)
        pb = jnp.zeros((nh, C_KV_HEADS * C_CMP_HID), F32)
        for s in range(C_CMP_STRIDE):
            xs = jnp.concatenate([cp[kv][pg][pl.ds(s, per_page, stride=C_CMP_STRIDE), :] for pg in range(n_pages)],
                                 axis=0)
            pa = pa + _mm(xs + pxa_ref[kv, s], wxa_ref[kv, s])
            pb = pb + _mm(xs + pxb_ref[kv, s], wxb_ref[kv, s])
        hid = jax.nn.silu(pa + pltpu.roll(pb, nh - 1, axis=0))
        cmpv += [_mm(hid[:, hd * C_CMP_HID:(hd + 1) * C_CMP_HID], w2_ref[kv]) for hd in range(C_KV_HEADS)]

    qpos_t = q_start + lax.broadcasted_iota(jnp.int32, (ts, 1), 0)
    qpos = jnp.concatenate([qpos_t] * C_REP, axis=0)
    cend = C_CMP_STRIDE * lax.broadcasted_iota(jnp.int32, (1, nh), 1) + (C_CMP_LEN - 1)
    maskc = cend <= qpos
    distc = (qpos - cend).astype(F32)
    zq = jnp.zeros((rows, C_HD), F32)

    qg, slope, o_c, scores = [], [], [], []
    for g in range(C_KV_HEADS):
        heads = [g * C_REP + r for r in range(C_REP)]
        qg.append(jnp.concatenate([q_ref[:, h * C_HD:(h + 1) * C_HD] for h in heads], axis=0) * (C_HD ** -0.5))
        slope.append(jnp.concatenate([jnp.full((ts, 1), 2.0 ** -(h + 1), F32) for h in heads], axis=0))
        p = _msoftmax(_mm_nt(qg[g], cmpv[g]) - slope[g] * distc, maskc)
        o_c.append(_mm(p, cmpv[C_KV_HEADS + g]))
        pm = jnp.dot(p.astype(BF16), mcs_ref[...], preferred_element_type=F32)
        imp = pm[0:ts]
        for r in range(1, C_REP):
            imp = imp + pm[r * ts:(r + 1) * ts]
        scores.append(_block_scores(imp, qpos_t))
    pad = jnp.full((QB - C_KV_HEADS * ts, LANES), NEG, F32)
    st_ref[...] = jnp.concatenate(scores + [pad], axis=0).T
    sel = _topn_mask(st_ref, nblk)

    def softmax_pv(tiles, lanes):
        m = tiles[0][0].max(axis=-1, keepdims=True)
        for s, _, _ in tiles[1:]:
            m = jnp.maximum(m, s.max(axis=-1, keepdims=True))
        l = jnp.zeros((rows, 1), F32)
        acc = jnp.zeros((rows, KV_ROW), F32)
        for s, msk, kv in tiles:
            e = jnp.where(msk, jnp.exp(s - m), 0.0)
            l = l + jnp.sum(e, axis=-1, keepdims=True)
            acc = acc + _mm(e, kv)
        return acc[:, lanes] / jnp.maximum(l, 1e-30)

    kpos_new = q_start + lax.broadcasted_iota(jnp.int32, (1, ts), 1)
    outs = []
    for g in range(C_KV_HEADS):
        qz = jnp.concatenate([qg[g] if sl == g else zq for sl in range(2 * C_KV_HEADS)], axis=1)
        vl = slice((C_KV_HEADS + g) * C_HD, (C_KV_HEADS + g + 1) * C_HD)
        selg = jnp.concatenate([sel[g * ts:(g + 1) * ts]] * C_REP, axis=0)
        mx = jnp.dot(selg.astype(BF16), e_ref[...], preferred_element_type=F32)
        tiles = []
        for pg in range(n_pages):
            kv = sp[pg][...]
            kpos = pg * page + lax.broadcasted_iota(jnp.int32, (1, page), 1)
            dist = qpos - kpos
            msk = jnp.where(dist >= 0, mx[:, pg * page:(pg + 1) * page], 0.0) > 0.5
            tiles.append((jnp.where(msk, _mm_nt(qz, kv) - slope[g] * dist.astype(F32), NEG), msk, kv))
        kv = new_ref[:, KV_ROW:2 * KV_ROW]
        dist = qpos - kpos_new
        msk = jnp.where(dist >= 0, mx[:, n_pages * page:n_pages * page + ts], 0.0) > 0.5
        tiles.append((jnp.where(msk, _mm_nt(qz, kv) - slope[g] * dist.astype(F32), NEG), msk, kv))
        o_s = softmax_pv(tiles, vl)

        tiles = []
        for w0 in range(0, n_win, KT):
            kv = win_ref[w0:w0 + KT, :]
            kpos = (q_start - n_win + w0) + lax.broadcasted_iota(jnp.int32, (1, KT), 1)
            dist = qpos - kpos
            msk = dist.astype(jnp.uint32) < C_WINDOW
            tiles.append((jnp.where(msk, _mm_nt(qz, kv) - slope[g] * dist.astype(F32), NEG), msk, kv))
        kv = new_ref[:, 2 * KV_ROW:3 * KV_ROW]
        dist = qpos - kpos_new
        msk = dist.astype(jnp.uint32) < C_WINDOW
        tiles.append((jnp.where(msk, _mm_nt(qz, kv) - slope[g] * dist.astype(F32), NEG), msk, kv))
        o_w = softmax_pv(tiles, vl)

        gates = jax.nn.sigmoid(gate_ref[:, g * LANES:(g + 1) * LANES])
        for r in range(C_REP):
            rs = slice(r * ts, (r + 1) * ts)
            outs.append(gates[:, r:r + 1] * o_c[g][rs] + gates[:, C_REP + r:C_REP + r + 1] * o_s[rs]
                        + gates[:, 2 * C_REP + r:2 * C_REP + r + 1] * o_w[rs])
    o_ref[...] = jnp.concatenate(outs, axis=1)


def _nsa_sample(p, pool_cmp, pool_sel, win_buf, page_table, layer, cwx, nb, ts, past_len):
    wxa, wxb, pxa, pxb, w2 = cwx
    n_pages = page_table.shape[1]
    page = past_len // n_pages
    nh = past_len // C_CMP_STRIDE
    n_win = win_buf.shape[1]
    assert (past_len + ts) // C_CMP_STRIDE == nh and past_len % KT == 0 and n_win % KT == 0 and ts <= KT
    nblk = -(-(-(-(past_len + ts) // C_SEL_LEN)) // 8) * 8
    mcs = _cmp_to_sel(nh, LANES).astype(BF16)
    e = _block_expand(past_len + KT)
    cst = lambda a: pl.BlockSpec(a.shape, lambda b, pt: (0,) * a.ndim)
    half = KV_ROW // 2
    pg_spec = lambda pg: pl.BlockSpec((None, page, KV_ROW), lambda b, pt: (pt[b, pg], layer, 0))
    half_spec = lambda pg, kv: pl.BlockSpec((None, page, half), lambda b, pt: (pt[b, pg], layer, kv))
    page_specs = ([half_spec(pg, 0) for pg in range(n_pages)] + [half_spec(pg, 1) for pg in range(n_pages)]
                  + [pg_spec(pg) for pg in range(n_pages)])
    consts = (wxa, wxb, pxa, pxb, w2, mcs, e)
    grid_spec = pltpu.PrefetchScalarGridSpec(
        num_scalar_prefetch=1,
        grid=(nb,),
        in_specs=[pl.BlockSpec((ts, C_Q), lambda b, pt: (b, 0)),
                  pl.BlockSpec((ts, C_KV_HEADS * LANES), lambda b, pt: (b, OD_G // (C_KV_HEADS * LANES))),
                  pl.BlockSpec((ts, C_KV), lambda b, pt: (b, OD_KV // C_KV)),
                  pl.BlockSpec((None, n_win, KV_ROW), lambda b, pt: (b, 0, 0))]
                 + [cst(a) for a in consts] + page_specs,
        out_specs=pl.BlockSpec((ts, C_Q), lambda b, pt: (b, 0)),
        scratch_shapes=[pltpu.VMEM((LANES, QB), F32)],
    )
    return pl.pallas_call(
        functools.partial(_nsa_sample_kernel, n_pages=n_pages, page=page, ts=ts, q_start=past_len, nblk=nblk),
        grid_spec=grid_spec,
        out_shape=jax.ShapeDtypeStruct((nb * ts, C_Q), F32),
        compiler_params=_cparams(("parallel",)),
        name="nsa_sample",
    )(page_table, p, p, p, win_buf, *consts, *([pool_cmp] * (2 * n_pages)), *([pool_sel] * n_pages))


def _odd_mix_sample(x, g_pre, w_in, cw, dw, pool_cmp, pool_sel, win_buf, page_table, layer, nb, ts, past_len):
    _, rowwise = cw
    p = _in_proj(x, g_pre, w_in)
    od, dv = _chunk_mlp(p, *dw, nb * ts, ts)
    kv_new = p[:, OD_KV:OD_KV + C_KV]
    oc = _nsa_sample(p, pool_cmp, pool_sel, win_buf.reshape(nb, win_buf.shape[1], KV_ROW), page_table, layer,
                     rowwise, nb, ts, past_len)
    return oc, od, kv_new, dv


def kernel(x_prompt, x_sample, state_delta, state_conv_a, state_conv_b, cache_cmp_kv, cache_sel_kv, cache_win_kv, page_table, norm_g, ffn_gate, ffn_up, ffn_down, ev_w_in, ev_w_out, ev_a_conv, ev_a_log, ev_dt_bias, ev_a_norm, ev_b_conv, od_w_in, od_w_out, od_cmp_pe, od_cmp_w1, od_cmp_w2, od_d_ws, od_d_bs, od_d_ln_g, od_d_ln_b):
    depth = norm_g.shape[0]
    n_even = (depth + 1) // 2
    n_odd = depth // 2
    bf = lambda a: a.astype(BF16)
    bp, tp, _ = x_prompt.shape
    bs, ts, _ = x_sample.shape
    page = cache_cmp_kv.shape[2]
    n_pages = page_table.shape[1]
    past_len = n_pages * page
    n_pool = cache_cmp_kv.shape[0]
    pool_cmp = cache_cmp_kv.reshape(n_pool, n_odd * page, C_KV // 3)
    pool_sel = cache_sel_kv.reshape(n_pool, n_odd * page, C_KV // 3)

    xp = x_prompt.reshape(bp * tp, D_MODEL)
    xs = x_sample.reshape(bs * ts, D_MODEL)
    zeros = lambda *s: jnp.zeros(s, F32)
    res = {k: [] for k in ("dp", "ds", "cap", "cas", "cbp", "cbs", "cmpp", "cmps", "selp", "sels", "winp", "wins", "dvs")}
    for li in range(depth):
        j = li // 2
        g = norm_g[li]
        ffn1 = (bf(ffn_gate[li, 0]), bf(ffn_up[li, 0]), bf(ffn_down[li, 0]))
        ffn2 = (bf(ffn_gate[li, 1]), bf(ffn_up[li, 1]), bf(ffn_down[li, 1]))
        xp = _ffn_half(xp, g[0], g[1], *ffn1)
        xs = _ffn_half(xs, g[0], g[1], *ffn1)
        if li % 2 == 0:
            w_in = bf(_even_w_in(ev_w_in[j]))
            w_out = bf(ev_w_out[j])
            mixw = (ev_a_conv[j], ev_a_log[j], ev_dt_bias[j], ev_a_norm[j], ev_b_conv[j])
            oa, ob, s, ca, cb = _even_mix(xp, g[2], w_in, zeros(bp, A_HEADS, A_DK, A_DV), zeros(bp, A_CONV - 1, A_CONV_DIM),
                                          zeros(bp, B_CONV - 1, B_WIDTH), *mixw, bp, tp)
            xp = _out_proj(xp, oa, ob, g[3], w_out)
            res["dp"].append(s); res["cap"].append(ca); res["cbp"].append(cb)
            oa, ob, s, ca, cb = _even_mix(xs, g[2], w_in, state_delta[:, j], state_conv_a[:, j], state_conv_b[:, j],
                                          *mixw, bs, ts)
            xs = _out_proj(xs, oa, ob, g[3], w_out)
            res["ds"].append(s); res["cas"].append(ca); res["cbs"].append(cb)
        else:
            w_in = bf(_odd_w_in(od_w_in[j]))
            w_out = bf(od_w_out[j])
            cw = _compress_weights(od_cmp_pe[j], od_cmp_w1[j], od_cmp_w2[j])
            dw = (od_d_ws[j], od_d_bs[j], od_d_ln_g[j], od_d_ln_b[j])
            oc, od, kv_new, _ = _odd_mix_prompt(xp, g[2], w_in, cw, dw, bp, tp)
            xp = _out_proj(xp, oc, od, g[3], w_out)
            kv_new = kv_new.reshape(bp, tp, 3, 2, C_KV_HEADS, C_HD)
            res["cmpp"].append(kv_new[:, :, 0]); res["selp"].append(kv_new[:, :, 1])
            res["winp"].append(kv_new[:, tp - min(C_WINDOW, tp):, 2])
            oc, od, kv_new, dv = _odd_mix_sample(xs, g[2], w_in, cw, dw, pool_cmp, pool_sel, cache_win_kv[:, j],
                                                 page_table, j, bs, ts, past_len)
            xs = _out_proj(xs, oc, od, g[3], w_out)
            kv_new = kv_new.reshape(bs, ts, 3, 2, C_KV_HEADS, C_HD)
            res["cmps"].append(kv_new[:, :, 0]); res["sels"].append(kv_new[:, :, 1])
            win_all = jnp.concatenate([cache_win_kv[:, j], kv_new[:, :, 2]], axis=1)
            res["wins"].append(win_all[:, win_all.shape[1] - min(C_WINDOW, past_len + ts):])
            res["dvs"].append(dv.reshape(bs, ts, D_WIDTH))
        xp = _ffn_half(xp, g[4], g[5], *ffn2)
        xs = _ffn_half(xs, g[4], g[5], *ffn2)
    st = lambda k: jnp.stack(res[k], axis=1)
    return (xp.reshape(bp, tp, D_MODEL), xs.reshape(bs, ts, D_MODEL), st("dp"), st("ds"), st("cap"), st("cas"),
            st("cbp"), st("cbs"), st("cmpp"), st("cmps"), st("selp"), st("sels"), st("winp"), st("wins"), st("dvs"))
```

```python
import functools

import jax
import jax.numpy as jnp
from jax import lax
from jax.experimental import pallas as pl
from jax.experimental.pallas import tpu as pltpu

F32 = jnp.float32
BF16 = jnp.bfloat16

D_MODEL = 1024
D_FF = 2816
EPS = 1e-6
A_HEADS = 4
A_DK = 128
A_DV = 128
A_CONV = 4
A_CHUNK = 64
A_QK = A_HEADS * A_DK
A_V = A_HEADS * A_DV
A_CONV_DIM = 2 * A_QK + A_V
B_WIDTH = 512
B_CONV = 3
C_HEADS = 8
C_KV_HEADS = 2
C_REP = C_HEADS // C_KV_HEADS
C_HD = 64
C_CMP_STRIDE = 16
C_CMP_LEN = 32
C_CMP_HID = 128
C_SEL_LEN = 64
C_TOPN = 16
C_WINDOW = 512
C_Q = C_HEADS * C_HD
C_KV = 3 * 2 * C_KV_HEADS * C_HD
D_GROUPS = 8
D_WIDTH = 512
D_CHUNK = 128
BIG = 1e4
NEG = -1e30

LANES = 128
VMEM_LIMIT = 56 * 1024 * 1024


def _cparams(sem):
    return pltpu.CompilerParams(dimension_semantics=sem, vmem_limit_bytes=VMEM_LIMIT)


def _rms(x, g):
    return x * lax.rsqrt(jnp.mean(x * x, axis=-1, keepdims=True) + EPS) * g


def _mm(a, b):
    return jnp.dot(a.astype(BF16), b.astype(BF16), preferred_element_type=F32)


def _mm_nt(a, b):
    return lax.dot_general(a.astype(BF16), b.astype(BF16), (((1,), (1,)), ((), ())),
                           preferred_element_type=F32)


def _mm_tn(a, b):
    return lax.dot_general(a.astype(BF16), b.astype(BF16), (((0,), (0,)), ((), ())),
                           preferred_element_type=F32)


def _row_tile(n, pref):
    t = min(n, pref)
    while n % t:
        t //= 2
    return t


def _const_spec(shape):
    return pl.BlockSpec(shape, lambda *_: (0,) * len(shape))


def _ffn_kernel(x_ref, gpre_ref, gpost_ref, wg_ref, wu_ref, wd_ref, o_ref, *, ff_chunk):
    x = x_ref[...]
    h = _rms(x, gpre_ref[...]).astype(BF16)
    acc = jnp.zeros(x.shape, F32)
    for c in range(D_FF // ff_chunk):
        sl = slice(c * ff_chunk, (c + 1) * ff_chunk)
        a = jnp.dot(h, wg_ref[:, sl], preferred_element_type=F32)
        u = jnp.dot(h, wu_ref[:, sl], preferred_element_type=F32)
        act = (jax.nn.silu(a) * u).astype(BF16)
        acc = acc + jnp.dot(act, wd_ref[sl, :], preferred_element_type=F32)
    o_ref[...] = x + 0.5 * _rms(acc, gpost_ref[...])


def _ffn_half(x, g_pre, g_post, wg, wu, wd):
    n = x.shape[0]
    tm = _row_tile(n, 512)
    row = pl.BlockSpec((tm, D_MODEL), lambda i: (i, 0))
    return pl.pallas_call(
        functools.partial(_ffn_kernel, ff_chunk=D_FF // 2),
        grid=(n // tm,),
        in_specs=[row, _const_spec((1, D_MODEL)), _const_spec((1, D_MODEL)),
                  _const_spec((D_MODEL, D_FF)), _const_spec((D_MODEL, D_FF)), _const_spec((D_FF, D_MODEL))],
        out_specs=row,
        out_shape=jax.ShapeDtypeStruct((n, D_MODEL), F32),
        compiler_params=_cparams(("parallel",)),
        name="ffn_half",
    )(x, g_pre.reshape(1, -1), g_post.reshape(1, -1), wg, wu, wd)


def _proj_kernel(x_ref, g_ref, w_ref, o_ref):
    h = _rms(x_ref[...], g_ref[...]).astype(BF16)
    o_ref[...] = jnp.dot(h, w_ref[...], preferred_element_type=F32)


def _in_proj(x, g, w):
    n = x.shape[0]
    cols = w.shape[1]
    tm = _row_tile(n, 512)
    return pl.pallas_call(
        _proj_kernel,
        grid=(n // tm,),
        in_specs=[pl.BlockSpec((tm, D_MODEL), lambda i: (i, 0)), _const_spec((1, D_MODEL)),
                  _const_spec((D_MODEL, cols))],
        out_specs=pl.BlockSpec((tm, cols), lambda i: (i, 0)),
        out_shape=jax.ShapeDtypeStruct((n, cols), F32),
        compiler_params=_cparams(("parallel",)),
        name="in_proj",
    )(x, g.reshape(1, -1), w)


def _out_proj_kernel(x_ref, oa_ref, ob_ref, g_ref, w_ref, o_ref):
    half = oa_ref.shape[1]
    y = (jnp.dot(oa_ref[...].astype(BF16), w_ref[:half, :], preferred_element_type=F32)
         + jnp.dot(ob_ref[...].astype(BF16), w_ref[half:, :], preferred_element_type=F32))
    o_ref[...] = x_ref[...] + _rms(y, g_ref[...])


def _out_proj(x, oa, ob, g, w):
    n = x.shape[0]
    tm = _row_tile(n, 512)
    half = oa.shape[1]
    return pl.pallas_call(
        _out_proj_kernel,
        grid=(n // tm,),
        in_specs=[pl.BlockSpec((tm, D_MODEL), lambda i: (i, 0)),
                  pl.BlockSpec((tm, half), lambda i: (i, 0)), pl.BlockSpec((tm, half), lambda i: (i, 0)),
                  _const_spec((1, D_MODEL)), _const_spec((2 * half, D_MODEL))],
        out_specs=pl.BlockSpec((tm, D_MODEL), lambda i: (i, 0)),
        out_shape=jax.ShapeDtypeStruct((n, D_MODEL), F32),
        compiler_params=_cparams(("parallel",)),
        name="out_proj",
    )(x, oa, ob, g.reshape(1, -1), w)


EV_QKV = 0
EV_Z = A_CONV_DIM
EV_H = EV_Z + A_V
EV_GB = EV_H + B_WIDTH
EV_GC = EV_GB + B_WIDTH
EV_BA = EV_GC + B_WIDTH
EV_COLS = EV_BA + LANES
CARRY = 8


def _even_w_in(w):
    a_in = A_CONV_DIM + A_V + 2 * A_HEADS
    pad = jnp.zeros((w.shape[0], LANES - 2 * A_HEADS), w.dtype)
    return jnp.concatenate([w[:, :A_CONV_DIM + A_V], w[:, a_in:], w[:, A_CONV_DIM + A_V:a_in], pad], axis=1)


def _even_prep_kernel(qkv_ref, h_ref, gb_ref, gc_ref, ba_ref, ca0_ref, cb0_ref, wa_ref, wb_ref, alog_ref,
                      dtb_ref, q_ref, k_ref, v_ref, gbeta_ref, ob_ref, can_ref, cbn_ref, xa, xb, *, tt):
    j = pl.program_id(1)

    @pl.when(j == 0)
    def _():
        xa[CARRY - (A_CONV - 1):CARRY, :] = ca0_ref[...]
        xb[CARRY - (B_CONV - 1):CARRY, :] = cb0_ref[...]

    @pl.when(j > 0)
    def _():
        xa[0:CARRY, :] = xa[tt:tt + CARRY, :]
        xb[0:CARRY, :] = xb[tt:tt + CARRY, :]

    xa[CARRY:CARRY + tt, :] = qkv_ref[...]
    xb[CARRY:CARRY + tt, :] = gc_ref[...] * h_ref[...]

    ya = wa_ref[0:1, :] * xa[pl.ds(CARRY - 3, tt), :]
    for i in range(1, A_CONV):
        ya = ya + wa_ref[i:i + 1, :] * xa[pl.ds(CARRY - 3 + i, tt), :]
    yb = wb_ref[0:1, :] * xb[pl.ds(CARRY - 2, tt), :]
    for i in range(1, B_CONV):
        yb = yb + wb_ref[i:i + 1, :] * xb[pl.ds(CARRY - 2 + i, tt), :]

    ob_ref[...] = gb_ref[...] * yb
    can_ref[...] = xa[tt + CARRY - (A_CONV - 1):tt + CARRY, :]
    cbn_ref[...] = xb[tt + CARRY - (B_CONV - 1):tt + CARRY, :]

    ya = jax.nn.silu(ya)
    for hd in range(A_HEADS):
        cs = slice(hd * A_DK, (hd + 1) * A_DK)
        qh = ya[:, hd * A_DK:(hd + 1) * A_DK]
        kh = ya[:, A_QK + hd * A_DK:A_QK + (hd + 1) * A_DK]
        q_ref[:, cs] = qh * lax.rsqrt(jnp.sum(qh * qh, axis=-1, keepdims=True) + EPS) * (A_DK ** -0.5)
        k_ref[:, cs] = kh * lax.rsqrt(jnp.sum(kh * kh, axis=-1, keepdims=True) + EPS)
    v_ref[...] = ya[:, 2 * A_QK:]

    ba = ba_ref[...]
    lane = lax.broadcasted_iota(jnp.int32, ba.shape, 1)
    beta = jax.nn.sigmoid(ba)
    g = -jnp.exp(alog_ref[...]) * jax.nn.softplus(ba + dtb_ref[...])
    gbeta_ref[...] = jnp.where(lane < A_HEADS, beta, g)


def _even_prep(p, conv_a0, conv_b0, w_conv_a, w_conv_b, a_log, dt_bias, nb, t):
    tt = _row_tile(t, 512)
    nt = t // tt
    n = nb * t
    rows = lambda w, cb: pl.BlockSpec((tt, w), lambda b, j: (b * nt + j, cb))
    pad = lambda a: jnp.zeros((1, LANES), F32).at[0, A_HEADS:2 * A_HEADS].set(a)
    outs = pl.pallas_call(
        functools.partial(_even_prep_kernel, tt=tt),
        grid=(nb, nt),
        in_specs=[rows(A_CONV_DIM, 0), rows(B_WIDTH, EV_H // B_WIDTH), rows(B_WIDTH, EV_GB // B_WIDTH),
                  rows(B_WIDTH, EV_GC // B_WIDTH), rows(LANES, EV_BA // LANES),
                  pl.BlockSpec((None, A_CONV - 1, A_CONV_DIM), lambda b, j: (b, 0, 0)),
                  pl.BlockSpec((None, B_CONV - 1, B_WIDTH), lambda b, j: (b, 0, 0)),
                  _const_spec((A_CONV, A_CONV_DIM)), _const_spec((B_CONV, B_WIDTH)),
                  _const_spec((1, LANES)), _const_spec((1, LANES))],
        out_specs=[rows(A_QK, 0), rows(A_QK, 0), rows(A_V, 0), rows(LANES, 0), rows(B_WIDTH, 0),
                   pl.BlockSpec((None, A_CONV - 1, A_CONV_DIM), lambda b, j: (b, 0, 0)),
                   pl.BlockSpec((None, B_CONV - 1, B_WIDTH), lambda b, j: (b, 0, 0))],
        out_shape=[jax.ShapeDtypeStruct((n, A_QK), F32), jax.ShapeDtypeStruct((n, A_QK), F32),
                   jax.ShapeDtypeStruct((n, A_V), F32), jax.ShapeDtypeStruct((n, LANES), F32),
                   jax.ShapeDtypeStruct((n, B_WIDTH), F32),
                   jax.ShapeDtypeStruct((nb, A_CONV - 1, A_CONV_DIM), F32),
                   jax.ShapeDtypeStruct((nb, B_CONV - 1, B_WIDTH), F32)],
        scratch_shapes=[pltpu.VMEM((tt + CARRY, A_CONV_DIM), F32), pltpu.VMEM((tt + CARRY, B_WIDTH), F32)],
        compiler_params=_cparams(("parallel", "arbitrary")),
        name="even_prep",
    )(p, p, p, p, p, conv_a0, conv_b0, w_conv_a, w_conv_b, pad(a_log), pad(dt_bias))
    return outs


GB = 64
GP = A_HEADS * GB


def _segment_sums(x, c):
    n = x.shape[0]
    seg = jnp.bitwise_and(lax.broadcasted_iota(jnp.int32, x.shape, 0), c - 1)
    pre, suf = x, x
    s = 1
    while s < c:
        pre = pre + jnp.where(seg >= s, pltpu.roll(pre, s, axis=0), 0.0)
        suf = suf + jnp.where(seg < c - s, pltpu.roll(suf, n - s, axis=0), 0.0)
        s *= 2
    return pre, suf


def _unit_lower_inverse(a, eye, order):
    x = -a
    t = eye + x
    p = x
    s = 2
    while s < order:
        p = _mm(p, p)
        t = t + _mm(t, p)
        s *= 2
    return t


def _gdn_prep_kernel(q_ref, k_ref, v_ref, gb_ref, u_ref, w_ref, kt_ref, qd_ref, in_ref, dec_ref, *, c, cps):
    row = lax.broadcasted_iota(jnp.int32, (GP, GP), 0)
    col = lax.broadcasted_iota(jnp.int32, (GP, GP), 1)
    d = row - col
    lim = jnp.bitwise_and(row, c - 1).astype(jnp.uint32)
    incl = d.astype(jnp.uint32) <= lim
    strict = (d - 1).astype(jnp.uint32) < lim
    eye_b = d == 0
    eye = jnp.where(eye_b, 1.0, 0.0)
    for ci in range(cps):
        rs = slice(ci * GB, (ci + 1) * GB)
        ps = slice(ci * GP, (ci + 1) * GP)
        gbv = gb_ref[rs, :]
        pre, suf = _segment_sums(gbv, c)
        stack = lambda ref: jnp.concatenate([ref[rs, hd * A_DK:(hd + 1) * A_DK] for hd in range(A_HEADS)], axis=0)
        column = lambda a, off: jnp.concatenate([a[:, off + hd:off + hd + 1] for hd in range(A_HEADS)], axis=0)
        qs, ks, vs = stack(q_ref), stack(k_ref), stack(v_ref)
        beta = column(gbv, 0)
        gcol = column(pre, A_HEADS)
        rest = column(suf, A_HEADS) - column(gbv, A_HEADS)
        grow = jnp.sum(jnp.where(eye_b, gcol, 0.0), axis=0, keepdims=True)
        dmask = jnp.where(incl, jnp.exp(jnp.where(incl, gcol - grow, 0.0)), 0.0)
        kb = ks * beta
        a_mat = jnp.where(strict, _mm_nt(kb, ks) * dmask, 0.0)
        t_mat = _unit_lower_inverse(a_mat, eye, c)
        egc = jnp.exp(gcol)
        u_ref[ps, :] = _mm(t_mat, vs * beta)
        w_ref[ps, :] = _mm(t_mat, kb * egc).astype(w_ref.dtype)
        in_ref[ps, :] = jnp.where(incl, _mm_nt(qs, ks) * dmask, 0.0).astype(in_ref.dtype)
        kt_ref[ps, :] = (ks * jnp.exp(rest)).astype(kt_ref.dtype)
        qd_ref[ps, :] = (qs * egc).astype(qd_ref.dtype)
        dec_ref[ps, :] = jnp.broadcast_to(jnp.exp(gcol + rest), (GP, A_DV))


def _gdn_scan_kernel(u_ref, w_ref, kt_ref, qd_ref, in_ref, dec_ref, z_ref, gn_ref, s0_ref, oa_ref, s_ref, *, c, cps):
    @pl.when(pl.program_id(1) == 0)
    def _():
        s_ref[...] = s0_ref[...]

    nsub = GB // c
    gn = gn_ref[...]
    for ci in range(cps):
        base = ci * GP
        vnew, qs_o = [], []
        for hd in range(A_HEADS):
            for sub in range(nsub):
                r0 = base + hd * GB + sub * c
                s = s_ref[sub * A_HEADS + hd]
                vn = u_ref[r0:r0 + c, :] - _mm(w_ref[r0:r0 + c, :], s)
                qs_o.append(_mm(qd_ref[r0:r0 + c, :], s))
                s_ref[sub * A_HEADS + hd] = s * dec_ref[r0 + c - 1:r0 + c, :] + _mm_tn(kt_ref[r0:r0 + c, :], vn)
                vnew.append(vn)
        o = jnp.concatenate(qs_o, axis=0) + _mm(in_ref[base:base + GP, :], jnp.concatenate(vnew, axis=0))
        for hd in range(A_HEADS):
            rs = slice(ci * GB, (ci + 1) * GB)
            cs = slice(hd * A_DV, (hd + 1) * A_DV)
            oa_ref[rs, cs] = _rms(o[hd * GB:(hd + 1) * GB], gn) * jax.nn.silu(z_ref[rs, cs])


def _gdn(q, k, v, gbeta, p, g_norm, s0, nb, t):
    c = A_CHUNK if t >= A_CHUNK else t
    n = nb * t
    assert GB % c == 0 and t % c == 0 and n % GB == 0 and (t % GB == 0 or GB % t == 0)
    nsub = GB // c
    nblk = n // GB
    cps = 4 if t % (4 * GB) == 0 else 1
    opd = BF16 if c % 16 == 0 else F32
    rows = lambda w: pl.BlockSpec((cps * GB, w), lambda i: (i, 0))
    stk = lambda w: pl.BlockSpec((cps * GP, w), lambda i: (i, 0))
    u, w, kt, qd, intra, dec = pl.pallas_call(
        functools.partial(_gdn_prep_kernel, c=c, cps=cps),
        grid=(nblk // cps,),
        in_specs=[rows(A_QK), rows(A_QK), rows(A_V), rows(LANES)],
        out_specs=[stk(A_DV), stk(A_DK), stk(A_DK), stk(A_DK), stk(GP), stk(A_DV)],
        out_shape=[jax.ShapeDtypeStruct((nblk * GP, A_DV), F32), jax.ShapeDtypeStruct((nblk * GP, A_DK), opd),
                   jax.ShapeDtypeStruct((nblk * GP, A_DK), opd), jax.ShapeDtypeStruct((nblk * GP, A_DK), opd),
                   jax.ShapeDtypeStruct((nblk * GP, GP), BF16), jax.ShapeDtypeStruct((nblk * GP, A_DV), F32)],
        compiler_params=_cparams(("parallel",)),
        name="gdn_prep",
    )(q, k, v, gbeta)
    ngrp = nb // nsub if t < GB else nb
    nt = nblk // ngrp // cps
    stk2 = lambda w_: pl.BlockSpec((cps * GP, w_), lambda b, j: (b * nt + j, 0))
    st = pl.BlockSpec((None, nsub * A_HEADS, A_DK, A_DV), lambda b, j: (b, 0, 0, 0))
    oa, s = pl.pallas_call(
        functools.partial(_gdn_scan_kernel, c=c, cps=cps),
        grid=(ngrp, nt),
        in_specs=[stk2(A_DV), stk2(A_DK), stk2(A_DK), stk2(A_DK), stk2(GP), stk2(A_DV),
                  pl.BlockSpec((cps * GB, A_V), lambda b, j: (b * nt + j, EV_Z // A_V)),
                  _const_spec((1, A_DV)), st],
        out_specs=[pl.BlockSpec((cps * GB, A_V), lambda b, j: (b * nt + j, 0)), st],
        out_shape=[jax.ShapeDtypeStruct((n, A_V), F32),
                   jax.ShapeDtypeStruct((ngrp, nsub * A_HEADS, A_DK, A_DV), F32)],
        compiler_params=_cparams(("parallel", "arbitrary")),
        name="gdn_scan",
    )(u, w, kt, qd, intra, dec, p, g_norm.reshape(1, -1), s0.reshape(ngrp, nsub * A_HEADS, A_DK, A_DV))
    return oa, s.reshape(nb, A_HEADS, A_DK, A_DV)


def _even_mix(x, g_pre, w_in, s0, conv_a0, conv_b0, a_conv, a_log, dt_bias, a_norm, b_conv, nb, t):
    p = _in_proj(x, g_pre, w_in)
    q, k, v, gbeta, ob, conv_a, conv_b = _even_prep(p, conv_a0, conv_b0, a_conv, b_conv, a_log, dt_bias, nb, t)
    oa, s = _gdn(q, k, v, gbeta, p, a_norm, s0, nb, t)
    return oa, ob, s, conv_a, conv_b


OD_Q = 0
OD_U = C_Q
OD_V = OD_U + D_WIDTH
OD_KV = OD_V + D_WIDTH
OD_G = OD_KV + C_KV
OD_COLS = OD_G + C_KV_HEADS * LANES
KV_ROW = C_KV // 3
QB = 128
KT = 128
SEL_UNROLL = 4
WIN_TILES = C_WINDOW // KT + 1
K_AUG = 128


def _with_positions(k, pos):
    assert int(pos.shape[0]) <= 256 * C_SEL_LEN
    extra = jnp.stack([pos // C_SEL_LEN, pos % C_SEL_LEN, jnp.ones_like(pos), jnp.ones_like(pos)], axis=-1)
    extra = jnp.pad(extra.astype(BF16), ((0, 0), (0, K_AUG - C_HD - 4)))
    return jnp.concatenate([k, jnp.broadcast_to(extra, k.shape[:-1] + extra.shape[-1:])], axis=-1)


def _odd_w_in(w):
    d = w.shape[0]
    gw = w[:, C_Q:C_Q + 3 * C_HEADS].reshape(d, C_KV_HEADS, C_REP, 3)
    gw = jnp.transpose(gw, (0, 1, 3, 2)).reshape(d, C_KV_HEADS, 3 * C_REP)
    gw = jnp.pad(gw, ((0, 0), (0, 0), (0, LANES - 3 * C_REP))).reshape(d, C_KV_HEADS * LANES)
    kv0 = C_Q + 3 * C_HEADS
    return jnp.concatenate([w[:, :C_Q], w[:, kv0 + C_KV:], w[:, kv0:kv0 + C_KV], gw], axis=1)


def _compress_weights(pe, w1, w2):
    s = C_CMP_STRIDE
    flat = lambda a: a.reshape(2, s * C_HD, C_CMP_HID).astype(BF16)
    dense = (flat(w1[:, :s]), flat(w1[:, s:]), pe[:, :s].reshape(2, 1, s * C_HD), pe[:, s:].reshape(2, 1, s * C_HD),
             w2.astype(BF16))
    def rows(w1h, peh):
        wx = jnp.zeros((2, s, C_KV_HEADS * C_HD, C_KV_HEADS * C_CMP_HID), F32)
        for hd in range(C_KV_HEADS):
            wx = wx.at[:, :, hd * C_HD:(hd + 1) * C_HD, hd * C_CMP_HID:(hd + 1) * C_CMP_HID].set(w1h)
        px = jnp.concatenate([peh] * C_KV_HEADS, axis=-1)
        return wx.astype(BF16), px.reshape(2, s, 1, C_KV_HEADS * C_HD)

    wxa, pxa = rows(w1[:, :s], pe[:, :s])
    wxb, pxb = rows(w1[:, s:], pe[:, s:])
    return dense, (wxa, wxb, pxa, pxb, w2.astype(BF16))


def _cmp_to_sel(nc, ns):
    cs = C_CMP_STRIDE * jnp.arange(nc)[:, None]
    ss = C_SEL_LEN * jnp.arange(ns)[None, :]
    ov = jnp.clip(jnp.minimum(cs + C_CMP_LEN, ss + C_SEL_LEN) - jnp.maximum(cs, ss), 0, None)
    return ov.astype(F32) / C_CMP_LEN


def _block_expand(n_keys):
    return (jnp.arange(n_keys)[None, :] // C_SEL_LEN == jnp.arange(LANES)[:, None]).astype(BF16)


def _chunk_mlp_kernel(u_ref, v_ref, w_ref, bias_ref, lng_ref, lnb_ref, o_ref, vrow_ref, *, span):
    u = jax.nn.gelu(u_ref[...])
    v = jax.nn.gelu(v_ref[...])
    mu = jnp.mean(v, axis=-1, keepdims=True)
    var = jnp.mean(jnp.square(v - mu), axis=-1, keepdims=True)
    vn = (v - mu) * lax.rsqrt(var + EPS) * lng_ref[...] + lnb_ref[...]
    vrow_ref[...] = vn
    n = u.shape[0]
    row = lax.broadcasted_iota(jnp.int32, (n, n), 0)
    col = lax.broadcasted_iota(jnp.int32, (n, n), 1)
    keep = (row - col).astype(jnp.uint32) <= jnp.bitwise_and(row, span - 1).astype(jnp.uint32)
    lane = lax.broadcasted_iota(jnp.int32, (n, LANES), 1)
    gw = D_WIDTH // D_GROUPS
    for pr in range(D_GROUPS // 2):
        cs = slice(pr * LANES, (pr + 1) * LANES)
        vp = vn[:, cs]
        m0 = _mm(jnp.where(keep, w_ref[2 * pr], 0.0), vp)
        m1 = _mm(jnp.where(keep, w_ref[2 * pr + 1], 0.0), vp)
        o_ref[:, cs] = u[:, cs] * (jnp.where(lane < gw, m0, m1) + bias_ref[:, cs])


def _chunk_mlp(p, ws, bs, ln_g, ln_b, n, span):
    tile = D_CHUNK
    reps = tile // span
    wt = jnp.tile(ws[:, :span, :span], (1, reps, reps))
    bt = jnp.repeat(jnp.tile(bs[:, :span].T, (reps, 1)), D_WIDTH // D_GROUPS, axis=1)
    rows = lambda cb: pl.BlockSpec((tile, D_WIDTH), lambda i: (i, cb))
    return pl.pallas_call(
        functools.partial(_chunk_mlp_kernel, span=span),
        grid=(n // tile,),
        in_specs=[rows(OD_U // D_WIDTH), rows(OD_V // D_WIDTH), _const_spec((D_GROUPS, tile, tile)),
                  _const_spec((tile, D_WIDTH)), _const_spec((1, D_WIDTH)), _const_spec((1, D_WIDTH))],
        out_specs=[rows(0), rows(0)],
        out_shape=[jax.ShapeDtypeStruct((n, D_WIDTH), F32), jax.ShapeDtypeStruct((n, D_WIDTH), F32)],
        compiler_params=_cparams(("parallel",)),
        name="chunk_mlp",
    )(p, p, wt, bt, ln_g.reshape(1, -1), ln_b.reshape(1, -1))


def _compress_kernel(x_ref, w1a_ref, w1b_ref, pea_ref, peb_ref, w2_ref, o_ref):
    x = x_ref[...]
    nh = x.shape[0]
    pa = _mm(x + pea_ref[...], w1a_ref[...])
    pb = _mm(x + peb_ref[...], w1b_ref[...])
    hid = jax.nn.silu(pa + pltpu.roll(pb, nh - 1, axis=0))
    o_ref[...] = _mm(hid, w2_ref[...]).astype(o_ref.dtype)


def _compress(xc, cw):
    w1a, w1b, pea, peb, w2 = cw
    nb, n_slab, nh, width = xc.shape
    kvw = lambda shape: pl.BlockSpec((None,) + shape, lambda b, s: (s // C_KV_HEADS, 0, 0))
    return pl.pallas_call(
        _compress_kernel,
        grid=(nb, n_slab),
        in_specs=[pl.BlockSpec((None, None, nh, width), lambda b, s: (b, s, 0, 0)),
                  kvw((width, C_CMP_HID)), kvw((width, C_CMP_HID)), kvw((1, width)), kvw((1, width)),
                  kvw((C_CMP_HID, C_HD))],
        out_specs=pl.BlockSpec((None, None, nh, C_HD), lambda b, s: (b, s, 0, 0)),
        out_shape=jax.ShapeDtypeStruct((nb, n_slab, nh, C_HD), BF16),
        compiler_params=_cparams(("parallel", "parallel")),
        name="compress",
    )(xc, w1a, w1b, pea, peb, w2)


def _msoftmax(s, mask):
    s = jnp.where(mask, s, NEG)
    m = jnp.max(s, axis=-1, keepdims=True)
    e = jnp.where(mask, jnp.exp(s - m), 0.0)
    return e / jnp.maximum(jnp.sum(e, axis=-1, keepdims=True), 1e-30)


def _block_scores(imp, qpos):
    blk = lax.broadcasted_iota(jnp.int32, imp.shape, 1)
    cur = jnp.right_shift(qpos, 6)
    forced = (blk == 0) | (blk == cur) | (blk == cur - 1)
    return jnp.where(blk <= cur, jnp.where(forced, BIG, imp), NEG)


def _topn_rows(st_ref, nblk):
    st = st_ref[0:nblk, :]
    rowi = lax.broadcasted_iota(jnp.int32, st.shape, 0)
    rank = jnp.zeros(st.shape, F32)
    for s2 in range(nblk):
        b = jnp.broadcast_to(st_ref[s2:s2 + 1, :], st.shape)
        rank = rank + jnp.where(b > st, 1.0, jnp.where(b == st, jnp.where(rowi > s2, 1.0, 0.0), 0.0))
    return jnp.where(rank < C_TOPN, jnp.where(st > 0.5 * NEG, 1.0, 0.0), 0.0)


def _topn_mask(st_ref, nblk):
    sel_t = _topn_rows(st_ref, nblk)
    if nblk < LANES:
        sel_t = jnp.concatenate([sel_t, jnp.zeros((LANES - nblk, sel_t.shape[1]), F32)], axis=0)
    return sel_t.T


def _nsa_prompt_kernel(q_ref, gate_ref, kc_ref, vct_ref, ks_ref, vst_ref, kw_ref, vwt_ref, mcst_ref, et_ref,
                       o_ref, st_ref, s_ref, m_ref, l_ref, acc_ref, *, nh, nblk):
    g = pl.program_id(1)
    i = pl.program_id(2)
    qpos = i * QB + lax.broadcasted_iota(jnp.int32, (1, QB), 1)
    q_hi = jnp.right_shift(qpos, 6).astype(F32)
    q_lo = jnp.bitwise_and(qpos, C_SEL_LEN - 1).astype(F32)
    lanes = [slice(r * QB, (r + 1) * QB) for r in range(C_REP)]
    qt = q_ref[...].T
    sub = lax.broadcasted_iota(jnp.int32, (8, QB), 0)
    pos_rows = []
    for r in range(C_REP):
        slope = jnp.where(g == 0, 2.0 ** -(r + 1), 2.0 ** -(r + 1 + C_REP))
        pos_rows.append(jnp.where(sub == 0, 64.0 * slope, jnp.where(sub == 1, slope, jnp.where(
            sub == 2, -64.0 * slope * q_hi, jnp.where(sub == 3, -slope * q_lo, 0.0)))))
    q_t = jnp.concatenate([
        jnp.concatenate([qt[r * C_HD:(r + 1) * C_HD] for r in range(C_REP)], axis=1) * (C_HD ** -0.5),
        jnp.concatenate(pos_rows, axis=1),
        jnp.zeros((K_AUG - C_HD - 8, C_REP * QB), F32)], axis=0).astype(BF16)
    gate_t = jax.nn.sigmoid(gate_ref[...]).T

    distc = qpos - (C_CMP_STRIDE * lax.broadcasted_iota(jnp.int32, (nh, 1), 0) + (C_CMP_LEN - 1))
    maskc = distc >= 0
    sc = jnp.dot(kc_ref[...], q_t, preferred_element_type=F32)
    ps = []
    for r in range(C_REP):
        s = jnp.where(maskc, sc[:, lanes[r]], NEG)
        e = jnp.where(maskc, jnp.exp(s - jnp.max(s, axis=0, keepdims=True)), 0.0)
        ps.append((e / jnp.maximum(jnp.sum(e, axis=0, keepdims=True), 1e-30)).astype(BF16))
    o_c = jnp.dot(vct_ref[...], jnp.concatenate(ps, axis=1), preferred_element_type=F32)
    imp = jnp.dot(mcst_ref[...], ps[0], preferred_element_type=F32)
    for r in range(1, C_REP):
        imp = imp + jnp.dot(mcst_ref[...], ps[r], preferred_element_type=F32)

    blk = lax.broadcasted_iota(jnp.int32, (LANES, QB), 0)
    cur = jnp.right_shift(qpos, 6)
    forced = (blk == 0) | (blk == cur) | (blk == cur - 1)
    st_ref[...] = jnp.where(blk <= cur, jnp.where(forced, BIG, imp), NEG)
    sel_t = _topn_rows(st_ref, nblk).astype(BF16)

    key_row = lax.broadcasted_iota(jnp.int32, (KT, 1), 0)

    cat = lambda xs: jnp.concatenate(xs, axis=1)

    def group_update(k_ref, vt_ref, k0s, msks, state):
        m_old, l_old, acc = state
        mx = [m_old[:, lanes[r]] for r in range(C_REP)]
        for u, (k0, msk) in enumerate(zip(k0s, msks)):
            s_all = jnp.dot(k_ref[pl.ds(k0, KT), :], q_t, preferred_element_type=F32)
            for r in range(C_REP):
                s = jnp.where(msk, s_all[:, lanes[r]], NEG)
                s_ref[u * KT:(u + 1) * KT, lanes[r]] = s
                mx[r] = jnp.maximum(mx[r], jnp.max(s, axis=0, keepdims=True))
        m_new = cat(mx)
        alpha = jnp.exp(m_old - m_new)
        lsum = [jnp.zeros((1, QB), F32)] * C_REP
        pv = None
        for u, k0 in enumerate(k0s):
            ps = []
            for r in range(C_REP):
                p = jnp.exp(s_ref[u * KT:(u + 1) * KT, lanes[r]] - mx[r])
                lsum[r] = lsum[r] + jnp.sum(p, axis=0, keepdims=True)
                ps.append(p.astype(BF16))
            d = jnp.dot(vt_ref[:, pl.ds(k0, KT)], cat(ps), preferred_element_type=F32)
            pv = d if pv is None else pv + d
        return m_new, alpha * l_old + cat(lsum), alpha * acc + pv

    empty = (jnp.full((1, C_REP * QB), NEG, F32), jnp.zeros((1, C_REP * QB), F32),
             jnp.zeros((C_HD, C_REP * QB), F32))

    m_ref[...], l_ref[...], acc_ref[...] = empty

    def sel_body(jt, carry):
        k0s, msks = [], []
        for u in range(SEL_UNROLL):
            k0 = pl.multiple_of((jt * SEL_UNROLL + u) * KT, KT)
            chosen = jnp.dot(et_ref[pl.ds(k0, KT), :], sel_t, preferred_element_type=F32)
            k0s.append(k0)
            msks.append(jnp.where(qpos >= k0 + key_row, chosen, 0.0) > 0.5)
        m_ref[...], l_ref[...], acc_ref[...] = group_update(ks_ref, vst_ref, k0s, msks,
                                                            (m_ref[...], l_ref[...], acc_ref[...]))
        return carry

    lax.fori_loop(0, i // SEL_UNROLL + 1, sel_body, 0)
    o_s = acc_ref[...] / jnp.maximum(l_ref[...], 1e-30)

    k0s, msks = [], []
    for u in range(WIN_TILES):
        kt = i - C_WINDOW // KT + u
        kpos = kt * KT + key_row
        k0s.append(pl.multiple_of(jnp.maximum(kt, 0) * KT, KT))
        msks.append(jnp.where(kpos >= 0, qpos - kpos, -1).astype(jnp.uint32) < C_WINDOW)
    _, l_w, acc_w = group_update(kw_ref, vwt_ref, k0s, msks, empty)
    o_w = acc_w / jnp.maximum(l_w, 1e-30)

    outs = []
    for r in range(C_REP):
        outs.append(gate_t[r:r + 1] * o_c[:, lanes[r]] + gate_t[C_REP + r:C_REP + r + 1] * o_s[:, lanes[r]]
                    + gate_t[2 * C_REP + r:2 * C_REP + r + 1] * o_w[:, lanes[r]])
    o_ref[...] = jnp.concatenate(outs, axis=0).T


def _nsa_prompt(p, kc, vct, kk, vt, nb, t):
    assert t % (SEL_UNROLL * KT) == 0 and QB == KT
    nqb = t // QB
    nh = t // C_CMP_STRIDE
    ns = t // C_SEL_LEN
    nblk = -(-ns // 8) * 8
    mcst =_cmp_to_sel(nh, LANES).T.astype(BF16)
    et = _block_expand(t).T[:, :nblk]
    gw = C_REP * C_HD
    kspec = lambda br: pl.BlockSpec((None, None, None, t, K_AUG), lambda b, g, i: (b, br, g, 0, 0))
    vspec = lambda br: pl.BlockSpec((None, None, None, C_HD, t), lambda b, g, i: (b, br, g, 0, 0))
    return pl.pallas_call(
        functools.partial(_nsa_prompt_kernel, nh=nh, nblk=nblk),
        grid=(nb, C_KV_HEADS, nqb),
        in_specs=[pl.BlockSpec((QB, gw), lambda b, g, i: (b * nqb + i, g)),
                  pl.BlockSpec((QB, LANES), lambda b, g, i: (b * nqb + i, OD_G // LANES + g)),
                  pl.BlockSpec((None, None, nh, K_AUG), lambda b, g, i: (b, g, 0, 0)),
                  pl.BlockSpec((None, None, C_HD, nh), lambda b, g, i: (b, g, 0, 0)),
                  kspec(0), vspec(0), kspec(1), vspec(1),
                  _const_spec((LANES, nh)), _const_spec((t, nblk))],
        out_specs=pl.BlockSpec((QB, gw), lambda b, g, i: (b * nqb + i, g)),
        out_shape=jax.ShapeDtypeStruct((nb * t, C_Q), F32),
        scratch_shapes=[pltpu.VMEM((LANES, QB), F32),
                        pltpu.VMEM((max(SEL_UNROLL, WIN_TILES) * KT, C_REP * QB), F32),
                        pltpu.VMEM((1, C_REP * QB), F32), pltpu.VMEM((1, C_REP * QB), F32),
                        pltpu.VMEM((C_HD, C_REP * QB), F32)],
        compiler_params=_cparams(("parallel", "parallel", "arbitrary")),
        name="nsa_prompt",
    )(p, p, kc, vct, kk, vt, kk, vt, mcst, et)


def _odd_mix_prompt(x, g_pre, w_in, cw, dw, nb, t):
    dense, _ = cw
    p = _in_proj(x, g_pre, w_in)
    od, dv = _chunk_mlp(p, *dw, nb * t, min(t, D_CHUNK))
    kv_new = p[:, OD_KV:OD_KV + C_KV]
    kvr = kv_new.reshape(nb, t, 3, 2, C_KV_HEADS, C_HD)
    kk = jnp.transpose(kvr[:, :, 1:, 0], (0, 2, 3, 1, 4)).astype(BF16)
    kk = _with_positions(kk, jnp.arange(t, dtype=jnp.int32))
    vt = jnp.transpose(kvr[:, :, 1:, 1], (0, 2, 3, 4, 1)).astype(BF16)
    nh = t // C_CMP_STRIDE
    xc = kv_new[:, :KV_ROW].reshape(nb, nh, C_CMP_STRIDE, KV_ROW // C_HD, C_HD)
    xc = jnp.transpose(xc, (0, 3, 1, 2, 4)).reshape(nb, KV_ROW // C_HD, nh, C_CMP_STRIDE * C_HD)
    cmp = _compress(xc, dense)
    kc = _with_positions(cmp[:, :C_KV_HEADS], C_CMP_STRIDE * jnp.arange(nh, dtype=jnp.int32) + (C_CMP_LEN - 1))
    vct =jnp.transpose(cmp[:, C_KV_HEADS:], (0, 1, 3, 2))
    oc = _nsa_prompt(p, kc, vct, kk, vt, nb, t)
    return oc, od, kv_new, dv


def _nsa_sample_kernel(pt_ref, q_ref, gate_ref, new_ref, win_ref, wxa_ref, wxb_ref, pxa_ref, pxb_ref, w2_ref,
                       mcs_ref, e_ref, *rest, n_pages, page, ts, q_start, nblk):
    del pt_ref
    cp = (rest[:n_pages], rest[n_pages:2 * n_pages])
    sp = rest[2 * n_pages:3 * n_pages]
    o_ref, st_ref = rest[3 * n_pages:]
    nh = n_pages * page // C_CMP_STRIDE
    per_page = page // C_CMP_STRIDE
    n_win = win_ref.shape[0]
    rows = C_REP * ts

    cmpv = []
    for kv in range(2):
        pa = jnp.zeros((nh, C_KV_HEADS * C_CMP_HID), F32)
        pb = jnp.zeros((nh, C_KV_HEADS * C_CMP_HID), F32)
        for s in range(C_CMP_STRIDE):
            xs = jnp.concatenate([cp[kv][pg][pl.ds(s, per_page, stride=C_CMP_STRIDE), :] for pg in range(n_pages)],
                                 axis=0)
            pa = pa + _mm(xs + pxa_ref[kv, s], wxa_ref[kv, s])
            pb = pb + _mm(xs + pxb_ref[kv, s], wxb_ref[kv, s])
        hid = jax.nn.silu(pa + pltpu.roll(pb, nh - 1, axis=0))
        cmpv += [_mm(hid[:, hd * C_CMP_HID:(hd + 1) * C_CMP_HID], w2_ref[kv]) for hd in range(C_KV_HEADS)]

    qpos_t = q_start + lax.broadcasted_iota(jnp.int32, (ts, 1), 0)
    qpos = jnp.concatenate([qpos_t] * C_REP, axis=0)
    cend = C_CMP_STRIDE * lax.broadcasted_iota(jnp.int32, (1, nh), 1) + (C_CMP_LEN - 1)
    maskc = cend <= qpos
    distc = (qpos - cend).astype(F32)
    zq = jnp.zeros((rows, C_HD), F32)

    qg, slope, o_c, scores = [], [], [], []
    for g in range(C_KV_HEADS):
        heads = [g * C_REP + r for r in range(C_REP)]
        qg.append(jnp.concatenate([q_ref[:, h * C_HD:(h + 1) * C_HD] for h in heads], axis=0) * (C_HD ** -0.5))
        slope.append(jnp.concatenate([jnp.full((ts, 1), 2.0 ** -(h + 1), F32) for h in heads], axis=0))
        p = _msoftmax(_mm_nt(qg[g], cmpv[g]) - slope[g] * distc, maskc)
        o_c.append(_mm(p, cmpv[C_KV_HEADS + g]))
        pm = jnp.dot(p.astype(BF16), mcs_ref[...], preferred_element_type=F32)
        imp = pm[0:ts]
        for r in range(1, C_REP):
            imp = imp + pm[r * ts:(r + 1) * ts]
        scores.append(_block_scores(imp, qpos_t))
    pad = jnp.full((QB - C_KV_HEADS * ts, LANES), NEG, F32)
    st_ref[...] = jnp.concatenate(scores + [pad], axis=0).T
    sel = _topn_mask(st_ref, nblk)

    def softmax_pv(tiles, lanes):
        m = tiles[0][0].max(axis=-1, keepdims=True)
        for s, _, _ in tiles[1:]:
            m = jnp.maximum(m, s.max(axis=-1, keepdims=True))
        l = jnp.zeros((rows, 1), F32)
        acc = jnp.zeros((rows, KV_ROW), F32)
        for s, msk, kv in tiles:
            e = jnp.where(msk, jnp.exp(s - m), 0.0)
            l = l + jnp.sum(e, axis=-1, keepdims=True)
            acc = acc + _mm(e, kv)
        return acc[:, lanes] / jnp.maximum(l, 1e-30)

    kpos_new = q_start + lax.broadcasted_iota(jnp.int32, (1, ts), 1)
    outs = []
    for g in range(C_KV_HEADS):
        qz = jnp.concatenate([qg[g] if sl == g else zq for sl in range(2 * C_KV_HEADS)], axis=1)
        vl = slice((C_KV_HEADS + g) * C_HD, (C_KV_HEADS + g + 1) * C_HD)
        selg = jnp.concatenate([sel[g * ts:(g + 1) * ts]] * C_REP, axis=0)
        mx = jnp.dot(selg.astype(BF16), e_ref[...], preferred_element_type=F32)
        tiles = []
        for pg in range(n_pages):
            kv = sp[pg][...]
            kpos = pg * page + lax.broadcasted_iota(jnp.int32, (1, page), 1)
            dist = qpos - kpos
            msk = jnp.where(dist >= 0, mx[:, pg * page:(pg + 1) * page], 0.0) > 0.5
            tiles.append((jnp.where(msk, _mm_nt(qz, kv) - slope[g] * dist.astype(F32), NEG), msk, kv))
        kv = new_ref[:, KV_ROW:2 * KV_ROW]
        dist = qpos - kpos_new
        msk = jnp.where(dist >= 0, mx[:, n_pages * page:n_pages * page + ts], 0.0) > 0.5
        tiles.append((jnp.where(msk, _mm_nt(qz, kv) - slope[g] * dist.astype(F32), NEG), msk, kv))
        o_s = softmax_pv(tiles, vl)

        tiles = []
        for w0 in range(0, n_win, KT):
            kv = win_ref[w0:w0 + KT, :]
            kpos = (q_start - n_win + w0) + lax.broadcasted_iota(jnp.int32, (1, KT), 1)
            dist = qpos - kpos
            msk = dist.astype(jnp.uint32) < C_WINDOW
            tiles.append((jnp.where(msk, _mm_nt(qz, kv) - slope[g] * dist.astype(F32), NEG), msk, kv))
        kv = new_ref[:, 2 * KV_ROW:3 * KV_ROW]
        dist = qpos - kpos_new
        msk = dist.astype(jnp.uint32) < C_WINDOW
        tiles.append((jnp.where(msk, _mm_nt(qz, kv) - slope[g] * dist.astype(F32), NEG), msk, kv))
        o_w = softmax_pv(tiles, vl)

        gates = jax.nn.sigmoid(gate_ref[:, g * LANES:(g + 1) * LANES])
        for r in range(C_REP):
            rs = slice(r * ts, (r + 1) * ts)
            outs.append(gates[:, r:r + 1] * o_c[g][rs] + gates[:, C_REP + r:C_REP + r + 1] * o_s[rs]
                        + gates[:, 2 * C_REP + r:2 * C_REP + r + 1] * o_w[rs])
    o_ref[...] = jnp.concatenate(outs, axis=1)


def _nsa_sample(p, pool_cmp, pool_sel, win_buf, page_table, layer, cwx, nb, ts, past_len):
    wxa, wxb, pxa, pxb, w2 = cwx
    n_pages = page_table.shape[1]
    page = past_len // n_pages
    nh = past_len // C_CMP_STRIDE
    n_win = win_buf.shape[1]
    assert (past_len + ts) // C_CMP_STRIDE == nh and past_len % KT == 0 and n_win % KT == 0 and ts <= KT
    nblk = -(-(-(-(past_len + ts) // C_SEL_LEN)) // 8) * 8
    mcs = _cmp_to_sel(nh, LANES).astype(BF16)
    e = _block_expand(past_len + KT)
    cst = lambda a: pl.BlockSpec(a.shape, lambda b, pt: (0,) * a.ndim)
    half = KV_ROW // 2
    pg_spec = lambda pg: pl.BlockSpec((None, page, KV_ROW), lambda b, pt: (pt[b, pg], layer, 0))
    half_spec = lambda pg, kv: pl.BlockSpec((None, page, half), lambda b, pt: (pt[b, pg], layer, kv))
    page_specs = ([half_spec(pg, 0) for pg in range(n_pages)] + [half_spec(pg, 1) for pg in range(n_pages)]
                  + [pg_spec(pg) for pg in range(n_pages)])
    consts = (wxa, wxb, pxa, pxb, w2, mcs, e)
    grid_spec = pltpu.PrefetchScalarGridSpec(
        num_scalar_prefetch=1,
        grid=(nb,),
        in_specs=[pl.BlockSpec((ts, C_Q), lambda b, pt: (b, 0)),
                  pl.BlockSpec((ts, C_KV_HEADS * LANES), lambda b, pt: (b, OD_G // (C_KV_HEADS * LANES))),
                  pl.BlockSpec((ts, C_KV), lambda b, pt: (b, OD_KV // C_KV)),
                  pl.BlockSpec((None, n_win, KV_ROW), lambda b, pt: (b, 0, 0))]
                 + [cst(a) for a in consts] + page_specs,
        out_specs=pl.BlockSpec((ts, C_Q), lambda b, pt: (b, 0)),
        scratch_shapes=[pltpu.VMEM((LANES, QB), F32)],
    )
    return pl.pallas_call(
        functools.partial(_nsa_sample_kernel, n_pages=n_pages, page=page, ts=ts, q_start=past_len, nblk=nblk),
        grid_spec=grid_spec,
        out_shape=jax.ShapeDtypeStruct((nb * ts, C_Q), F32),
        compiler_params=_cparams(("parallel",)),
        name="nsa_sample",
    )(page_table, p, p, p, win_buf, *consts, *([pool_cmp] * (2 * n_pages)), *([pool_sel] * n_pages))


def _odd_mix_sample(x, g_pre, w_in, cw, dw, pool_cmp, pool_sel, win_buf, page_table, layer, nb, ts, past_len):
    _, rowwise = cw
    p = _in_proj(x, g_pre, w_in)
    od, dv = _chunk_mlp(p, *dw, nb * ts, ts)
    kv_new = p[:, OD_KV:OD_KV + C_KV]
    oc = _nsa_sample(p, pool_cmp, pool_sel, win_buf.reshape(nb, win_buf.shape[1], KV_ROW), page_table, layer,
                     rowwise, nb, ts, past_len)
    return oc, od, kv_new, dv


def kernel(x_prompt, x_sample, state_delta, state_conv_a, state_conv_b, cache_cmp_kv, cache_sel_kv, cache_win_kv, page_table, norm_g, ffn_gate, ffn_up, ffn_down, ev_w_in, ev_w_out, ev_a_conv, ev_a_log, ev_dt_bias, ev_a_norm, ev_b_conv, od_w_in, od_w_out, od_cmp_pe, od_cmp_w1, od_cmp_w2, od_d_ws, od_d_bs, od_d_ln_g, od_d_ln_b):
    depth = norm_g.shape[0]
    n_even = (depth + 1) // 2
    n_odd = depth // 2
    bf = lambda a: a.astype(BF16)
    bp, tp, _ = x_prompt.shape
    bs, ts, _ = x_sample.shape
    page = cache_cmp_kv.shape[2]
    n_pages = page_table.shape[1]
    past_len = n_pages * page
    n_pool = cache_cmp_kv.shape[0]
    pool_cmp = cache_cmp_kv.reshape(n_pool, n_odd * page, C_KV // 3)
    pool_sel = cache_sel_kv.reshape(n_pool, n_odd * page, C_KV // 3)

    xp = x_prompt.reshape(bp * tp, D_MODEL)
    xs = x_sample.reshape(bs * ts, D_MODEL)
    zeros = lambda *s: jnp.zeros(s, F32)
    res = {k: [] for k in ("dp", "ds", "cap", "cas", "cbp", "cbs", "cmpp", "cmps", "selp", "sels", "winp", "wins", "dvs")}
    for li in range(depth):
        j = li // 2
        g = norm_g[li]
        ffn1 = (bf(ffn_gate[li, 0]), bf(ffn_up[li, 0]), bf(ffn_down[li, 0]))
        ffn2 = (bf(ffn_gate[li, 1]), bf(ffn_up[li, 1]), bf(ffn_down[li, 1]))
        xp = _ffn_half(xp, g[0], g[1], *ffn1)
        xs = _ffn_half(xs, g[0], g[1], *ffn1)
        if li % 2 == 0:
            w_in = bf(_even_w_in(ev_w_in[j]))
            w_out = bf(ev_w_out[j])
            mixw = (ev_a_conv[j], ev_a_log[j], ev_dt_bias[j], ev_a_norm[j], ev_b_conv[j])
            oa, ob, s, ca, cb = _even_mix(xp, g[2], w_in, zeros(bp, A_HEADS, A_DK, A_DV), zeros(bp, A_CONV - 1, A_CONV_DIM),
                                          zeros(bp, B_CONV - 1, B_WIDTH), *mixw, bp, tp)
            xp = _out_proj(xp, oa, ob, g[3], w_out)
            res["dp"].append(s); res["cap"].append(ca); res["cbp"].append(cb)
            oa, ob, s, ca, cb = _even_mix(xs, g[2], w_in, state_delta[:, j], state_conv_a[:, j], state_conv_b[:, j],
                                          *mixw, bs, ts)
            xs = _out_proj(xs, oa, ob, g[3], w_out)
            res["ds"].append(s); res["cas"].append(ca); res["cbs"].append(cb)
        else:
            w_in = bf(_odd_w_in(od_w_in[j]))
            w_out = bf(od_w_out[j])
            cw = _compress_weights(od_cmp_pe[j], od_cmp_w1[j], od_cmp_w2[j])
            dw = (od_d_ws[j], od_d_bs[j], od_d_ln_g[j], od_d_ln_b[j])
            oc, od, kv_new, _ = _odd_mix_prompt(xp, g[2], w_in, cw, dw, bp, tp)
            xp = _out_proj(xp, oc, od, g[3], w_out)
            kv_new = kv_new.reshape(bp, tp, 3, 2, C_KV_HEADS, C_HD)
            res["cmpp"].append(kv_new[:, :, 0]); res["selp"].append(kv_new[:, :, 1])
            res["winp"].append(kv_new[:, tp - min(C_WINDOW, tp):, 2])
            oc, od, kv_new, dv = _odd_mix_sample(xs, g[2], w_in, cw, dw, pool_cmp, pool_sel, cache_win_kv[:, j],
                                                 page_table, j, bs, ts, past_len)
            xs = _out_proj(xs, oc, od, g[3], w_out)
            kv_new = kv_new.reshape(bs, ts, 3, 2, C_KV_HEADS, C_HD)
            res["cmps"].append(kv_new[:, :, 0]); res["sels"].append(kv_new[:, :, 1])
            win_all = jnp.concatenate([cache_win_kv[:, j], kv_new[:, :, 2]], axis=1)
            res["wins"].append(win_all[:, win_all.shape[1] - min(C_WINDOW, past_len + ts):])
            res["dvs"].append(dv.reshape(bs, ts, D_WIDTH))
        xp = _ffn_half(xp, g[4], g[5], *ffn2)
        xs = _ffn_half(xs, g[4], g[5], *ffn2)
    st = lambda k: jnp.stack(res[k], axis=1)
    return (xp.reshape(bp, tp, D_MODEL), xs.reshape(bs, ts, D_MODEL), st("dp"), st("ds"), st("cap"), st("cas"),
            st("cbp"), st("cbs"), st("cmpp"), st("cmps"), st("selp"), st("sels"), st("winp"), st("wins"), st("dvs"))
```

```python
import functools

import jax
import jax.numpy as jnp
from jax import lax
from jax.experimental import pallas as pl
from jax.experimental.pallas import tpu as pltpu

F32 = jnp.float32
BF16 = jnp.bfloat16

D_MODEL = 1024
D_FF = 2816
EPS = 1e-6
A_HEADS = 4
A_DK = 128
A_DV = 128
A_CONV = 4
A_CHUNK = 64
A_QK = A_HEADS * A_DK
A_V = A_HEADS * A_DV
A_CONV_DIM = 2 * A_QK + A_V
B_WIDTH = 512
B_CONV = 3
C_HEADS = 8
C_KV_HEADS = 2
C_REP = C_HEADS // C_KV_HEADS
C_HD = 64
C_CMP_STRIDE = 16
C_CMP_LEN = 32
C_CMP_HID = 128
C_SEL_LEN = 64
C_TOPN = 16
C_WINDOW = 512
C_Q = C_HEADS * C_HD
C_KV = 3 * 2 * C_KV_HEADS * C_HD
D_GROUPS = 8
D_WIDTH = 512
D_CHUNK = 128
BIG = 1e4
NEG = -1e30

LANES = 128
VMEM_LIMIT = 56 * 1024 * 1024


def _cparams(sem):
    return pltpu.CompilerParams(dimension_semantics=sem, vmem_limit_bytes=VMEM_LIMIT)


def _rms(x, g):
    return x * lax.rsqrt(jnp.mean(x * x, axis=-1, keepdims=True) + EPS) * g


def _mm(a, b):
    return jnp.dot(a.astype(BF16), b.astype(BF16), preferred_element_type=F32)


def _mm_nt(a, b):
    return lax.dot_general(a.astype(BF16), b.astype(BF16), (((1,), (1,)), ((), ())),
                           preferred_element_type=F32)


def _mm_tn(a, b):
    return lax.dot_general(a.astype(BF16), b.astype(BF16), (((0,), (0,)), ((), ())),
                           preferred_element_type=F32)


def _row_tile(n, pref):
    t = min(n, pref)
    while n % t:
        t //= 2
    return t


def _const_spec(shape):
    return pl.BlockSpec(shape, lambda *_: (0,) * len(shape))


def _ffn_kernel(x_ref, gpre_ref, gpost_ref, wg_ref, wu_ref, wd_ref, o_ref, *, ff_chunk):
    x = x_ref[...]
    h = _rms(x, gpre_ref[...]).astype(BF16)
    acc = jnp.zeros(x.shape, F32)
    for c in range(D_FF // ff_chunk):
        sl = slice(c * ff_chunk, (c + 1) * ff_chunk)
        a = jnp.dot(h, wg_ref[:, sl], preferred_element_type=F32)
        u = jnp.dot(h, wu_ref[:, sl], preferred_element_type=F32)
        act = (jax.nn.silu(a) * u).astype(BF16)
        acc = acc + jnp.dot(act, wd_ref[sl, :], preferred_element_type=F32)
    o_ref[...] = x + 0.5 * _rms(acc, gpost_ref[...])


def _ffn_half(x, g_pre, g_post, wg, wu, wd):
    n = x.shape[0]
    tm = _row_tile(n, 512)
    row = pl.BlockSpec((tm, D_MODEL), lambda i: (i, 0))
    return pl.pallas_call(
        functools.partial(_ffn_kernel, ff_chunk=D_FF // 2),
        grid=(n // tm,),
        in_specs=[row, _const_spec((1, D_MODEL)), _const_spec((1, D_MODEL)),
                  _const_spec((D_MODEL, D_FF)), _const_spec((D_MODEL, D_FF)), _const_spec((D_FF, D_MODEL))],
        out_specs=row,
        out_shape=jax.ShapeDtypeStruct((n, D_MODEL), F32),
        compiler_params=_cparams(("parallel",)),
        name="ffn_half",
    )(x, g_pre.reshape(1, -1), g_post.reshape(1, -1), wg, wu, wd)


def _proj_kernel(x_ref, g_ref, w_ref, o_ref):
    h = _rms(x_ref[...], g_ref[...]).astype(BF16)
    o_ref[...] = jnp.dot(h, w_ref[...], preferred_element_type=F32)


def _in_proj(x, g, w):
    n = x.shape[0]
    cols = w.shape[1]
    tm = _row_tile(n, 512)
    return pl.pallas_call(
        _proj_kernel,
        grid=(n // tm,),
        in_specs=[pl.BlockSpec((tm, D_MODEL), lambda i: (i, 0)), _const_spec((1, D_MODEL)),
                  _const_spec((D_MODEL, cols))],
        out_specs=pl.BlockSpec((tm, cols), lambda i: (i, 0)),
        out_shape=jax.ShapeDtypeStruct((n, cols), F32),
        compiler_params=_cparams(("parallel",)),
        name="in_proj",
    )(x, g.reshape(1, -1), w)


def _out_proj_kernel(x_ref, oa_ref, ob_ref, g_ref, w_ref, o_ref):
    half = oa_ref.shape[1]
    y = (jnp.dot(oa_ref[...].astype(BF16), w_ref[:half, :], preferred_element_type=F32)
         + jnp.dot(ob_ref[...].astype(BF16), w_ref[half:, :], preferred_element_type=F32))
    o_ref[...] = x_ref[...] + _rms(y, g_ref[...])


def _out_proj(x, oa, ob, g, w):
    n = x.shape[0]
    tm = _row_tile(n, 512)
    half = oa.shape[1]
    return pl.pallas_call(
        _out_proj_kernel,
        grid=(n // tm,),
        in_specs=[pl.BlockSpec((tm, D_MODEL), lambda i: (i, 0)),
                  pl.BlockSpec((tm, half), lambda i: (i, 0)), pl.BlockSpec((tm, half), lambda i: (i, 0)),
                  _const_spec((1, D_MODEL)), _const_spec((2 * half, D_MODEL))],
        out_specs=pl.BlockSpec((tm, D_MODEL), lambda i: (i, 0)),
        out_shape=jax.ShapeDtypeStruct((n, D_MODEL), F32),
        compiler_params=_cparams(("parallel",)),
        name="out_proj",
    )(x, oa, ob, g.reshape(1, -1), w)


EV_QKV = 0
EV_Z = A_CONV_DIM
EV_H = EV_Z + A_V
EV_GB = EV_H + B_WIDTH
EV_GC = EV_GB + B_WIDTH
EV_BA = EV_GC + B_WIDTH
EV_COLS = EV_BA + LANES
CARRY = 8


def _even_w_in(w):
    a_in = A_CONV_DIM + A_V + 2 * A_HEADS
    pad = jnp.zeros((w.shape[0], LANES - 2 * A_HEADS), w.dtype)
    return jnp.concatenate([w[:, :A_CONV_DIM + A_V], w[:, a_in:], w[:, A_CONV_DIM + A_V:a_in], pad], axis=1)


def _even_prep_kernel(qkv_ref, h_ref, gb_ref, gc_ref, ba_ref, ca0_ref, cb0_ref, wa_ref, wb_ref, alog_ref,
                      dtb_ref, q_ref, k_ref, v_ref, gbeta_ref, ob_ref, can_ref, cbn_ref, xa, xb, *, tt):
    j = pl.program_id(1)

    @pl.when(j == 0)
    def _():
        xa[CARRY - (A_CONV - 1):CARRY, :] = ca0_ref[...]
        xb[CARRY - (B_CONV - 1):CARRY, :] = cb0_ref[...]

    @pl.when(j > 0)
    def _():
        xa[0:CARRY, :] = xa[tt:tt + CARRY, :]
        xb[0:CARRY, :] = xb[tt:tt + CARRY, :]

    xa[CARRY:CARRY + tt, :] = qkv_ref[...]
    xb[CARRY:CARRY + tt, :] = gc_ref[...] * h_ref[...]

    ya = wa_ref[0:1, :] * xa[pl.ds(CARRY - 3, tt), :]
    for i in range(1, A_CONV):
        ya = ya + wa_ref[i:i + 1, :] * xa[pl.ds(CARRY - 3 + i, tt), :]
    yb = wb_ref[0:1, :] * xb[pl.ds(CARRY - 2, tt), :]
    for i in range(1, B_CONV):
        yb = yb + wb_ref[i:i + 1, :] * xb[pl.ds(CARRY - 2 + i, tt), :]

    ob_ref[...] = gb_ref[...] * yb
    can_ref[...] = xa[tt + CARRY - (A_CONV - 1):tt + CARRY, :]
    cbn_ref[...] = xb[tt + CARRY - (B_CONV - 1):tt + CARRY, :]

    ya = jax.nn.silu(ya)
    for hd in range(A_HEADS):
        cs = slice(hd * A_DK, (hd + 1) * A_DK)
        qh = ya[:, hd * A_DK:(hd + 1) * A_DK]
        kh = ya[:, A_QK + hd * A_DK:A_QK + (hd + 1) * A_DK]
        q_ref[:, cs] = qh * lax.rsqrt(jnp.sum(qh * qh, axis=-1, keepdims=True) + EPS) * (A_DK ** -0.5)
        k_ref[:, cs] = kh * lax.rsqrt(jnp.sum(kh * kh, axis=-1, keepdims=True) + EPS)
    v_ref[...] = ya[:, 2 * A_QK:]

    ba = ba_ref[...]
    lane = lax.broadcasted_iota(jnp.int32, ba.shape, 1)
    beta = jax.nn.sigmoid(ba)
    g = -jnp.exp(alog_ref[...]) * jax.nn.softplus(ba + dtb_ref[...])
    gbeta_ref[...] = jnp.where(lane < A_HEADS, beta, g)


def _even_prep(p, conv_a0, conv_b0, w_conv_a, w_conv_b, a_log, dt_bias, nb, t):
    tt = _row_tile(t, 512)
    nt = t // tt
    n = nb * t
    rows = lambda w, cb: pl.BlockSpec((tt, w), lambda b, j: (b * nt + j, cb))
    pad = lambda a: jnp.zeros((1, LANES), F32).at[0, A_HEADS:2 * A_HEADS].set(a)
    outs = pl.pallas_call(
        functools.partial(_even_prep_kernel, tt=tt),
        grid=(nb, nt),
        in_specs=[rows(A_CONV_DIM, 0), rows(B_WIDTH, EV_H // B_WIDTH), rows(B_WIDTH, EV_GB // B_WIDTH),
                  rows(B_WIDTH, EV_GC // B_WIDTH), rows(LANES, EV_BA // LANES),
                  pl.BlockSpec((None, A_CONV - 1, A_CONV_DIM), lambda b, j: (b, 0, 0)),
                  pl.BlockSpec((None, B_CONV - 1, B_WIDTH), lambda b, j: (b, 0, 0)),
                  _const_spec((A_CONV, A_CONV_DIM)), _const_spec((B_CONV, B_WIDTH)),
                  _const_spec((1, LANES)), _const_spec((1, LANES))],
        out_specs=[rows(A_QK, 0), rows(A_QK, 0), rows(A_V, 0), rows(LANES, 0), rows(B_WIDTH, 0),
                   pl.BlockSpec((None, A_CONV - 1, A_CONV_DIM), lambda b, j: (b, 0, 0)),
                   pl.BlockSpec((None, B_CONV - 1, B_WIDTH), lambda b, j: (b, 0, 0))],
        out_shape=[jax.ShapeDtypeStruct((n, A_QK), F32), jax.ShapeDtypeStruct((n, A_QK), F32),
                   jax.ShapeDtypeStruct((n, A_V), F32), jax.ShapeDtypeStruct((n, LANES), F32),
                   jax.ShapeDtypeStruct((n, B_WIDTH), F32),
                   jax.ShapeDtypeStruct((nb, A_CONV - 1, A_CONV_DIM), F32),
                   jax.ShapeDtypeStruct((nb, B_CONV - 1, B_WIDTH), F32)],
        scratch_shapes=[pltpu.VMEM((tt + CARRY, A_CONV_DIM), F32), pltpu.VMEM((tt + CARRY, B_WIDTH), F32)],
        compiler_params=_cparams(("parallel", "arbitrary")),
        name="even_prep",
    )(p, p, p, p, p, conv_a0, conv_b0, w_conv_a, w_conv_b, pad(a_log), pad(dt_bias))
    return outs


GB = 64
GP = A_HEADS * GB


def _segment_sums(x, c):
    n = x.shape[0]
    seg = jnp.bitwise_and(lax.broadcasted_iota(jnp.int32, x.shape, 0), c - 1)
    pre, suf = x, x
    s = 1
    while s < c:
        pre = pre + jnp.where(seg >= s, pltpu.roll(pre, s, axis=0), 0.0)
        suf = suf + jnp.where(seg < c - s, pltpu.roll(suf, n - s, axis=0), 0.0)
        s *= 2
    return pre, suf


def _unit_lower_inverse(a, eye, order):
    x = -a
    t = eye + x
    p = x
    s = 2
    while s < order:
        p = _mm(p, p)
        t = t + _mm(t, p)
        s *= 2
    return t


def _gdn_prep_kernel(q_ref, k_ref, v_ref, gb_ref, u_ref, w_ref, kt_ref, qd_ref, in_ref, dec_ref, *, c, cps):
    row = lax.broadcasted_iota(jnp.int32, (GP, GP), 0)
    col = lax.broadcasted_iota(jnp.int32, (GP, GP), 1)
    d = row - col
    lim = jnp.bitwise_and(row, c - 1).astype(jnp.uint32)
    incl = d.astype(jnp.uint32) <= lim
    strict = (d - 1).astype(jnp.uint32) < lim
    eye_b = d == 0
    eye = jnp.where(eye_b, 1.0, 0.0)
    for ci in range(cps):
        rs = slice(ci * GB, (ci + 1) * GB)
        ps = slice(ci * GP, (ci + 1) * GP)
        gbv = gb_ref[rs, :]
        pre, suf = _segment_sums(gbv, c)
        stack = lambda ref: jnp.concatenate([ref[rs, hd * A_DK:(hd + 1) * A_DK] for hd in range(A_HEADS)], axis=0)
        column = lambda a, off: jnp.concatenate([a[:, off + hd:off + hd + 1] for hd in range(A_HEADS)], axis=0)
        qs, ks, vs = stack(q_ref), stack(k_ref), stack(v_ref)
        beta = column(gbv, 0)
        gcol = column(pre, A_HEADS)
        rest = column(suf, A_HEADS) - column(gbv, A_HEADS)
        grow = jnp.sum(jnp.where(eye_b, gcol, 0.0), axis=0, keepdims=True)
        dmask = jnp.where(incl, jnp.exp(jnp.where(incl, gcol - grow, 0.0)), 0.0)
        kb = ks * beta
        a_mat = jnp.where(strict, _mm_nt(kb, ks) * dmask, 0.0)
        t_mat = _unit_lower_inverse(a_mat, eye, c)
        egc = jnp.exp(gcol)
        u_ref[ps, :] = _mm(t_mat, vs * beta)
        w_ref[ps, :] = _mm(t_mat, kb * egc).astype(w_ref.dtype)
        in_ref[ps, :] = jnp.where(incl, _mm_nt(qs, ks) * dmask, 0.0).astype(in_ref.dtype)
        kt_ref[ps, :] = (ks * jnp.exp(rest)).astype(kt_ref.dtype)
        qd_ref[ps, :] = (qs * egc).astype(qd_ref.dtype)
        dec_ref[ps, :] = jnp.broadcast_to(jnp.exp(gcol + rest), (GP, A_DV))


def _gdn_scan_kernel(u_ref, w_ref, kt_ref, qd_ref, in_ref, dec_ref, z_ref, gn_ref, s0_ref, oa_ref, s_ref, *, c, cps):
    @pl.when(pl.program_id(1) == 0)
    def _():
        s_ref[...] = s0_ref[...]

    nsub = GB // c
    gn = gn_ref[...]
    for ci in range(cps):
        base = ci * GP
        vnew, qs_o = [], []
        for hd in range(A_HEADS):
            for sub in range(nsub):
                r0 = base + hd * GB + sub * c
                s = s_ref[sub * A_HEADS + hd]
                vn = u_ref[r0:r0 + c, :] - _mm(w_ref[r0:r0 + c, :], s)
                qs_o.append(_mm(qd_ref[r0:r0 + c, :], s))
                s_ref[sub * A_HEADS + hd] = s * dec_ref[r0 + c - 1:r0 + c, :] + _mm_tn(kt_ref[r0:r0 + c, :], vn)
                vnew.append(vn)
        o = jnp.concatenate(qs_o, axis=0) + _mm(in_ref[base:base + GP, :], jnp.concatenate(vnew, axis=0))
        for hd in range(A_HEADS):
            rs = slice(ci * GB, (ci + 1) * GB)
            cs = slice(hd * A_DV, (hd + 1) * A_DV)
            oa_ref[rs, cs] = _rms(o[hd * GB:(hd + 1) * GB], gn) * jax.nn.silu(z_ref[rs, cs])


def _gdn(q, k, v, gbeta, p, g_norm, s0, nb, t):
    c = A_CHUNK if t >= A_CHUNK else t
    n = nb * t
    assert GB % c == 0 and t % c == 0 and n % GB == 0 and (t % GB == 0 or GB % t == 0)
    nsub = GB // c
    nblk = n // GB
    cps = 4 if t % (4 * GB) == 0 else 1
    opd = BF16 if c % 16 == 0 else F32
    rows = lambda w: pl.BlockSpec((cps * GB, w), lambda i: (i, 0))
    stk = lambda w: pl.BlockSpec((cps * GP, w), lambda i: (i, 0))
    u, w, kt, qd, intra, dec = pl.pallas_call(
        functools.partial(_gdn_prep_kernel, c=c, cps=cps),
        grid=(nblk // cps,),
        in_specs=[rows(A_QK), rows(A_QK), rows(A_V), rows(LANES)],
        out_specs=[stk(A_DV), stk(A_DK), stk(A_DK), stk(A_DK), stk(GP), stk(A_DV)],
        out_shape=[jax.ShapeDtypeStruct((nblk * GP, A_DV), F32), jax.ShapeDtypeStruct((nblk * GP, A_DK), opd),
                   jax.ShapeDtypeStruct((nblk * GP, A_DK), opd), jax.ShapeDtypeStruct((nblk * GP, A_DK), opd),
                   jax.ShapeDtypeStruct((nblk * GP, GP), BF16), jax.ShapeDtypeStruct((nblk * GP, A_DV), F32)],
        compiler_params=_cparams(("parallel",)),
        name="gdn_prep",
    )(q, k, v, gbeta)
    ngrp = nb // nsub if t < GB else nb
    nt = nblk // ngrp // cps
    stk2 = lambda w_: pl.BlockSpec((cps * GP, w_), lambda b, j: (b * nt + j, 0))
    st = pl.BlockSpec((None, nsub * A_HEADS, A_DK, A_DV), lambda b, j: (b, 0, 0, 0))
    oa, s = pl.pallas_call(
        functools.partial(_gdn_scan_kernel, c=c, cps=cps),
        grid=(ngrp, nt),
        in_specs=[stk2(A_DV), stk2(A_DK), stk2(A_DK), stk2(A_DK), stk2(GP), stk2(A_DV),
                  pl.BlockSpec((cps * GB, A_V), lambda b, j: (b * nt + j, EV_Z // A_V)),
                  _const_spec((1, A_DV)), st],
        out_specs=[pl.BlockSpec((cps * GB, A_V), lambda b, j: (b * nt + j, 0)), st],
        out_shape=[jax.ShapeDtypeStruct((n, A_V), F32),
                   jax.ShapeDtypeStruct((ngrp, nsub * A_HEADS, A_DK, A_DV), F32)],
        compiler_params=_cparams(("parallel", "arbitrary")),
        name="gdn_scan",
    )(u, w, kt, qd, intra, dec, p, g_norm.reshape(1, -1), s0.reshape(ngrp, nsub * A_HEADS, A_DK, A_DV))
    return oa, s.reshape(nb, A_HEADS, A_DK, A_DV)


def _even_mix(x, g_pre, w_in, s0, conv_a0, conv_b0, a_conv, a_log, dt_bias, a_norm, b_conv, nb, t):
    p = _in_proj(x, g_pre, w_in)
    q, k, v, gbeta, ob, conv_a, conv_b = _even_prep(p, conv_a0, conv_b0, a_conv, b_conv, a_log, dt_bias, nb, t)
    oa, s = _gdn(q, k, v, gbeta, p, a_norm, s0, nb, t)
    return oa, ob, s, conv_a, conv_b


OD_Q = 0
OD_U = C_Q
OD_V = OD_U + D_WIDTH
OD_KV = OD_V + D_WIDTH
OD_G = OD_KV + C_KV
OD_COLS = OD_G + C_KV_HEADS * LANES
KV_ROW = C_KV // 3
QB = 128
KT = 128
SEL_UNROLL = 8
WIN_TILES = C_WINDOW // KT + 1
K_AUG = 128


def _with_positions(k, pos):
    assert int(pos.shape[0]) <= 256 * C_SEL_LEN
    extra = jnp.stack([pos // C_SEL_LEN, pos % C_SEL_LEN, jnp.ones_like(pos), jnp.ones_like(pos)], axis=-1)
    extra = jnp.pad(extra.astype(BF16), ((0, 0), (0, K_AUG - C_HD - 4)))
    return jnp.concatenate([k, jnp.broadcast_to(extra, k.shape[:-1] + extra.shape[-1:])], axis=-1)


def _odd_w_in(w):
    d = w.shape[0]
    gw = w[:, C_Q:C_Q + 3 * C_HEADS].reshape(d, C_KV_HEADS, C_REP, 3)
    gw = jnp.transpose(gw, (0, 1, 3, 2)).reshape(d, C_KV_HEADS, 3 * C_REP)
    gw = jnp.pad(gw, ((0, 0), (0, 0), (0, LANES - 3 * C_REP))).reshape(d, C_KV_HEADS * LANES)
    kv0 = C_Q + 3 * C_HEADS
    return jnp.concatenate([w[:, :C_Q], w[:, kv0 + C_KV:], w[:, kv0:kv0 + C_KV], gw], axis=1)


def _compress_weights(pe, w1, w2):
    s = C_CMP_STRIDE
    flat = lambda a: a.reshape(2, s * C_HD, C_CMP_HID).astype(BF16)
    dense = (flat(w1[:, :s]), flat(w1[:, s:]), pe[:, :s].reshape(2, 1, s * C_HD), pe[:, s:].reshape(2, 1, s * C_HD),
             w2.astype(BF16))
    def rows(w1h, peh):
        wx = jnp.zeros((2, s, C_KV_HEADS * C_HD, C_KV_HEADS * C_CMP_HID), F32)
        for hd in range(C_KV_HEADS):
            wx = wx.at[:, :, hd * C_HD:(hd + 1) * C_HD, hd * C_CMP_HID:(hd + 1) * C_CMP_HID].set(w1h)
        px = jnp.concatenate([peh] * C_KV_HEADS, axis=-1)
        return wx.astype(BF16), px.reshape(2, s, 1, C_KV_HEADS * C_HD)

    wxa, pxa = rows(w1[:, :s], pe[:, :s])
    wxb, pxb = rows(w1[:, s:], pe[:, s:])
    return dense, (wxa, wxb, pxa, pxb, w2.astype(BF16))


def _cmp_to_sel(nc, ns):
    cs = C_CMP_STRIDE * jnp.arange(nc)[:, None]
    ss = C_SEL_LEN * jnp.arange(ns)[None, :]
    ov = jnp.clip(jnp.minimum(cs + C_CMP_LEN, ss + C_SEL_LEN) - jnp.maximum(cs, ss), 0, None)
    return ov.astype(F32) / C_CMP_LEN


def _block_expand(n_keys):
    return (jnp.arange(n_keys)[None, :] // C_SEL_LEN == jnp.arange(LANES)[:, None]).astype(BF16)


def _chunk_mlp_kernel(u_ref, v_ref, w_ref, bias_ref, lng_ref, lnb_ref, o_ref, vrow_ref, *, span):
    u = jax.nn.gelu(u_ref[...])
    v = jax.nn.gelu(v_ref[...])
    mu = jnp.mean(v, axis=-1, keepdims=True)
    var = jnp.mean(jnp.square(v - mu), axis=-1, keepdims=True)
    vn = (v - mu) * lax.rsqrt(var + EPS) * lng_ref[...] + lnb_ref[...]
    vrow_ref[...] = vn
    n = u.shape[0]
    row = lax.broadcasted_iota(jnp.int32, (n, n), 0)
    col = lax.broadcasted_iota(jnp.int32, (n, n), 1)
    keep = (row - col).astype(jnp.uint32) <= jnp.bitwise_and(row, span - 1).astype(jnp.uint32)
    lane = lax.broadcasted_iota(jnp.int32, (n, LANES), 1)
    gw = D_WIDTH // D_GROUPS
    for pr in range(D_GROUPS // 2):
        cs = slice(pr * LANES, (pr + 1) * LANES)
        vp = vn[:, cs]
        m0 = _mm(jnp.where(keep, w_ref[2 * pr], 0.0), vp)
        m1 = _mm(jnp.where(keep, w_ref[2 * pr + 1], 0.0), vp)
        o_ref[:, cs] = u[:, cs] * (jnp.where(lane < gw, m0, m1) + bias_ref[:, cs])


def _chunk_mlp(p, ws, bs, ln_g, ln_b, n, span):
    tile = D_CHUNK
    reps = tile // span
    wt = jnp.tile(ws[:, :span, :span], (1, reps, reps))
    bt = jnp.repeat(jnp.tile(bs[:, :span].T, (reps, 1)), D_WIDTH // D_GROUPS, axis=1)
    rows = lambda cb: pl.BlockSpec((tile, D_WIDTH), lambda i: (i, cb))
    return pl.pallas_call(
        functools.partial(_chunk_mlp_kernel, span=span),
        grid=(n // tile,),
        in_specs=[rows(OD_U // D_WIDTH), rows(OD_V // D_WIDTH), _const_spec((D_GROUPS, tile, tile)),
                  _const_spec((tile, D_WIDTH)), _const_spec((1, D_WIDTH)), _const_spec((1, D_WIDTH))],
        out_specs=[rows(0), rows(0)],
        out_shape=[jax.ShapeDtypeStruct((n, D_WIDTH), F32), jax.ShapeDtypeStruct((n, D_WIDTH), F32)],
        compiler_params=_cparams(("parallel",)),
        name="chunk_mlp",
    )(p, p, wt, bt, ln_g.reshape(1, -1), ln_b.reshape(1, -1))


def _compress_kernel(x_ref, w1a_ref, w1b_ref, pea_ref, peb_ref, w2_ref, o_ref):
    x = x_ref[...]
    nh = x.shape[0]
    pa = _mm(x + pea_ref[...], w1a_ref[...])
    pb = _mm(x + peb_ref[...], w1b_ref[...])
    hid = jax.nn.silu(pa + pltpu.roll(pb, nh - 1, axis=0))
    o_ref[...] = _mm(hid, w2_ref[...]).astype(o_ref.dtype)


def _compress(xc, cw):
    w1a, w1b, pea, peb, w2 = cw
    nb, n_slab, nh, width = xc.shape
    kvw = lambda shape: pl.BlockSpec((None,) + shape, lambda b, s: (s // C_KV_HEADS, 0, 0))
    return pl.pallas_call(
        _compress_kernel,
        grid=(nb, n_slab),
        in_specs=[pl.BlockSpec((None, None, nh, width), lambda b, s: (b, s, 0, 0)),
                  kvw((width, C_CMP_HID)), kvw((width, C_CMP_HID)), kvw((1, width)), kvw((1, width)),
                  kvw((C_CMP_HID, C_HD))],
        out_specs=pl.BlockSpec((None, None, nh, C_HD), lambda b, s: (b, s, 0, 0)),
        out_shape=jax.ShapeDtypeStruct((nb, n_slab, nh, C_HD), BF16),
        compiler_params=_cparams(("parallel", "parallel")),
        name="compress",
    )(xc, w1a, w1b, pea, peb, w2)


def _msoftmax(s, mask):
    s = jnp.where(mask, s, NEG)
    m = jnp.max(s, axis=-1, keepdims=True)
    e = jnp.where(mask, jnp.exp(s - m), 0.0)
    return e / jnp.maximum(jnp.sum(e, axis=-1, keepdims=True), 1e-30)


def _block_scores(imp, qpos):
    blk = lax.broadcasted_iota(jnp.int32, imp.shape, 1)
    cur = jnp.right_shift(qpos, 6)
    forced = (blk == 0) | (blk == cur) | (blk == cur - 1)
    return jnp.where(blk <= cur, jnp.where(forced, BIG, imp), NEG)


def _topn_rows(st_ref, nblk):
    sub = 8
    rowi = lax.broadcasted_iota(jnp.int32, (sub, st_ref.shape[1]), 0)
    sel = []
    for k0 in range(0, nblk, sub):
        st = st_ref[k0:k0 + sub, :]
        rank = jnp.zeros(st.shape, F32)
        for s2 in range(nblk):
            b = jnp.broadcast_to(st_ref[s2:s2 + 1, :], st.shape)
            if s2 < k0:
                rank = rank + jnp.where(b >= st, 1.0, 0.0)
            elif s2 >= k0 + sub - 1:
                rank = rank + jnp.where(b > st, 1.0, 0.0)
            else:
                rank = rank + jnp.where(b > st, 1.0, jnp.where(b == st, jnp.where(rowi > s2 - k0, 1.0, 0.0), 0.0))
        sel.append(jnp.where(rank < C_TOPN, jnp.where(st > 0.5 * NEG, 1.0, 0.0), 0.0))
    return jnp.concatenate(sel, axis=0)


def _topn_mask(st_ref, nblk):
    sel_t = _topn_rows(st_ref, nblk)
    if nblk < LANES:
        sel_t = jnp.concatenate([sel_t, jnp.zeros((LANES - nblk, sel_t.shape[1]), F32)], axis=0)
    return sel_t.T


def _nsa_prompt_kernel(q_ref, gate_ref, kc_ref, vct_ref, ks_ref, vst_ref, kw_ref, vwt_ref, mcst_ref, et_ref,
                       o_ref, st_ref, s_ref, m_ref, l_ref, acc_ref, *, nh, nblk):
    g = pl.program_id(1)
    i = pl.program_id(2)
    qpos = i * QB + lax.broadcasted_iota(jnp.int32, (1, QB), 1)
    q_hi = jnp.right_shift(qpos, 6).astype(F32)
    q_lo = jnp.bitwise_and(qpos, C_SEL_LEN - 1).astype(F32)
    lanes = [slice(r * QB, (r + 1) * QB) for r in range(C_REP)]
    qt = q_ref[...].T
    sub = lax.broadcasted_iota(jnp.int32, (8, QB), 0)
    pos_rows = []
    for r in range(C_REP):
        slope = jnp.where(g == 0, 2.0 ** -(r + 1), 2.0 ** -(r + 1 + C_REP))
        pos_rows.append(jnp.where(sub == 0, 64.0 * slope, jnp.where(sub == 1, slope, jnp.where(
            sub == 2, -64.0 * slope * q_hi, jnp.where(sub == 3, -slope * q_lo, 0.0)))))
    q_t = jnp.concatenate([
        jnp.concatenate([qt[r * C_HD:(r + 1) * C_HD] for r in range(C_REP)], axis=1) * (C_HD ** -0.5),
        jnp.concatenate(pos_rows, axis=1),
        jnp.zeros((K_AUG - C_HD - 8, C_REP * QB), F32)], axis=0).astype(BF16)
    gate_t = jax.nn.sigmoid(gate_ref[...]).T

    distc = qpos - (C_CMP_STRIDE * lax.broadcasted_iota(jnp.int32, (nh, 1), 0) + (C_CMP_LEN - 1))
    maskc = distc >= 0
    sc = jnp.dot(kc_ref[...], q_t, preferred_element_type=F32)
    ps = []
    for r in range(C_REP):
        s = jnp.where(maskc, sc[:, lanes[r]], NEG)
        e = jnp.where(maskc, jnp.exp(s - jnp.max(s, axis=0, keepdims=True)), 0.0)
        ps.append((e / jnp.maximum(jnp.sum(e, axis=0, keepdims=True), 1e-30)).astype(BF16))
    o_c = jnp.dot(vct_ref[...], jnp.concatenate(ps, axis=1), preferred_element_type=F32)
    imp = jnp.dot(mcst_ref[...], ps[0], preferred_element_type=F32)
    for r in range(1, C_REP):
        imp = imp + jnp.dot(mcst_ref[...], ps[r], preferred_element_type=F32)

    blk = lax.broadcasted_iota(jnp.int32, (LANES, QB), 0)
    cur = jnp.right_shift(qpos, 6)
    forced = (blk == 0) | (blk == cur) | (blk == cur - 1)
    st_ref[...] = jnp.where(blk <= cur, jnp.where(forced, BIG, imp), NEG)
    sel_t = _topn_rows(st_ref, nblk).astype(BF16)

    key_row = lax.broadcasted_iota(jnp.int32, (KT, 1), 0)

    cat = lambda xs: jnp.concatenate(xs, axis=1)

    def group_update(k_ref, vt_ref, k0s, msks, state):
        m_old, l_old, acc = state
        mx = [m_old[:, lanes[r]] for r in range(C_REP)]
        for u, (k0, msk) in enumerate(zip(k0s, msks)):
            s_all = jnp.dot(k_ref[pl.ds(k0, KT), :], q_t, preferred_element_type=F32)
            for r in range(C_REP):
                s = jnp.where(msk, s_all[:, lanes[r]], NEG)
                s_ref[u * KT:(u + 1) * KT, lanes[r]] = s
                mx[r] = jnp.maximum(mx[r], jnp.max(s, axis=0, keepdims=True))
        m_new = cat(mx)
        alpha = jnp.exp(m_old - m_new)
        lsum = [jnp.zeros((1, QB), F32)] * C_REP
        pv = None
        for u, k0 in enumerate(k0s):
            ps = []
            for r in range(C_REP):
                p = jnp.exp(s_ref[u * KT:(u + 1) * KT, lanes[r]] - mx[r])
                lsum[r] = lsum[r] + jnp.sum(p, axis=0, keepdims=True)
                ps.append(p.astype(BF16))
            d = jnp.dot(vt_ref[:, pl.ds(k0, KT)], cat(ps), preferred_element_type=F32)
            pv = d if pv is None else pv + d
        return m_new, alpha * l_old + cat(lsum), alpha * acc + pv

    empty = (jnp.full((1, C_REP * QB), NEG, F32), jnp.zeros((1, C_REP * QB), F32),
             jnp.zeros((C_HD, C_REP * QB), F32))

    m_ref[...], l_ref[...], acc_ref[...] = empty

    def sel_body(jt, carry):
        k0s, msks = [], []
        for u in range(SEL_UNROLL):
            k0 = pl.multiple_of((jt * SEL_UNROLL + u) * KT, KT)
            chosen = jnp.dot(et_ref[pl.ds(k0, KT), :], sel_t, preferred_element_type=F32)
            k0s.append(k0)
            msks.append(jnp.where(qpos >= k0 + key_row, chosen, 0.0) > 0.5)
        m_ref[...], l_ref[...], acc_ref[...] = group_update(ks_ref, vst_ref, k0s, msks,
                                                            (m_ref[...], l_ref[...], acc_ref[...]))
        return carry

    lax.fori_loop(0, i // SEL_UNROLL + 1, sel_body, 0)
    o_s = acc_ref[...] / jnp.maximum(l_ref[...], 1e-30)

    k0s, msks = [], []
    for u in range(WIN_TILES):
        kt = i - C_WINDOW // KT + u
        kpos = kt * KT + key_row
        k0s.append(pl.multiple_of(jnp.maximum(kt, 0) * KT, KT))
        msks.append(jnp.where(kpos >= 0, qpos - kpos, -1).astype(jnp.uint32) < C_WINDOW)
    _, l_w, acc_w = group_update(kw_ref, vwt_ref, k0s, msks, empty)
    o_w = acc_w / jnp.maximum(l_w, 1e-30)

    outs = []
    for r in range(C_REP):
        outs.append(gate_t[r:r + 1] * o_c[:, lanes[r]] + gate_t[C_REP + r:C_REP + r + 1] * o_s[:, lanes[r]]
                    + gate_t[2 * C_REP + r:2 * C_REP + r + 1] * o_w[:, lanes[r]])
    o_ref[...] = jnp.concatenate(outs, axis=0).T


def _nsa_prompt(p, kc, vct, kk, vt, nb, t):
    assert t % (SEL_UNROLL * KT) == 0 and QB == KT
    nqb = t // QB
    nh = t // C_CMP_STRIDE
    ns = t // C_SEL_LEN
    nblk = -(-ns // 8) * 8
    mcst =_cmp_to_sel(nh, LANES).T.astype(BF16)
    et = _block_expand(t).T[:, :nblk]
    gw = C_REP * C_HD
    kspec = lambda br: pl.BlockSpec((None, None, None, t, K_AUG), lambda b, g, i: (b, br, g, 0, 0))
    vspec = lambda br: pl.BlockSpec((None, None, None, C_HD, t), lambda b, g, i: (b, br, g, 0, 0))
    return pl.pallas_call(
        functools.partial(_nsa_prompt_kernel, nh=nh, nblk=nblk),
        grid=(nb, C_KV_HEADS, nqb),
        in_specs=[pl.BlockSpec((QB, gw), lambda b, g, i: (b * nqb + i, g)),
                  pl.BlockSpec((QB, LANES), lambda b, g, i: (b * nqb + i, OD_G // LANES + g)),
                  pl.BlockSpec((None, None, nh, K_AUG), lambda b, g, i: (b, g, 0, 0)),
                  pl.BlockSpec((None, None, C_HD, nh), lambda b, g, i: (b, g, 0, 0)),
                  kspec(0), vspec(0), kspec(1), vspec(1),
                  _const_spec((LANES, nh)), _const_spec((t, nblk))],
        out_specs=pl.BlockSpec((QB, gw), lambda b, g, i: (b * nqb + i, g)),
        out_shape=jax.ShapeDtypeStruct((nb * t, C_Q), F32),
        scratch_shapes=[pltpu.VMEM((LANES, QB), F32),
                        pltpu.VMEM((max(SEL_UNROLL, WIN_TILES) * KT, C_REP * QB), F32),
                        pltpu.VMEM((1, C_REP * QB), F32), pltpu.VMEM((1, C_REP * QB), F32),
                        pltpu.VMEM((C_HD, C_REP * QB), F32)],
        compiler_params=_cparams(("parallel", "parallel", "arbitrary")),
        name="nsa_prompt",
    )(p, p, kc, vct, kk, vt, kk, vt, mcst, et)


def _odd_mix_prompt(x, g_pre, w_in, cw, dw, nb, t):
    dense, _ = cw
    p = _in_proj(x, g_pre, w_in)
    od, dv = _chunk_mlp(p, *dw, nb * t, min(t, D_CHUNK))
    kv_new = p[:, OD_KV:OD_KV + C_KV]
    kvr = kv_new.reshape(nb, t, 3, 2, C_KV_HEADS, C_HD)
    kk = jnp.transpose(kvr[:, :, 1:, 0], (0, 2, 3, 1, 4)).astype(BF16)
    kk = _with_positions(kk, jnp.arange(t, dtype=jnp.int32))
    vt = jnp.transpose(kvr[:, :, 1:, 1], (0, 2, 3, 4, 1)).astype(BF16)
    nh = t // C_CMP_STRIDE
    xc = kv_new[:, :KV_ROW].reshape(nb, nh, C_CMP_STRIDE, KV_ROW // C_HD, C_HD)
    xc = jnp.transpose(xc, (0, 3, 1, 2, 4)).reshape(nb, KV_ROW // C_HD, nh, C_CMP_STRIDE * C_HD)
    cmp = _compress(xc, dense)
    kc = _with_positions(cmp[:, :C_KV_HEADS], C_CMP_STRIDE * jnp.arange(nh, dtype=jnp.int32) + (C_CMP_LEN - 1))
    vct =jnp.transpose(cmp[:, C_KV_HEADS:], (0, 1, 3, 2))
    oc = _nsa_prompt(p, kc, vct, kk, vt, nb, t)
    return oc, od, kv_new, dv


def _nsa_sample_kernel(pt_ref, q_ref, gate_ref, new_ref, win_ref, wxa_ref, wxb_ref, pxa_ref, pxb_ref, w2_ref,
                       mcs_ref, e_ref, *rest, n_pages, page, ts, q_start, nblk):
    del pt_ref
    cp = (rest[:n_pages], rest[n_pages:2 * n_pages])
    sp = rest[2 * n_pages:3 * n_pages]
    o_ref, st_ref = rest[3 * n_pages:]
    nh = n_pages * page // C_CMP_STRIDE
    per_page = page // C_CMP_STRIDE
    n_win = win_ref.shape[0]
    rows = C_REP * ts

    cmpv = []
    for kv in range(2):
        pa = jnp.zeros((nh, C_KV_HEADS * C_CMP_HID), F32)
        pb = jnp.zeros((nh, C_KV_HEADS * C_CMP_HID), F32)
        for s in range(C_CMP_STRIDE):
            xs = jnp.concatenate([cp[kv][pg][pl.ds(s, per_page, stride=C_CMP_STRIDE), :] for pg in range(n_pages)],
                                 axis=0)
            pa = pa + _mm(xs + pxa_ref[kv, s], wxa_ref[kv, s])
            pb = pb + _mm(xs + pxb_ref[kv, s], wxb_ref[kv, s])
        hid = jax.nn.silu(pa + pltpu.roll(pb, nh - 1, axis=0))
        cmpv += [_mm(hid[:, hd * C_CMP_HID:(hd + 1) * C_CMP_HID], w2_ref[kv]) for hd in range(C_KV_HEADS)]

    qpos_t = q_start + lax.broadcasted_iota(jnp.int32, (ts, 1), 0)
    qpos = jnp.concatenate([qpos_t] * C_REP, axis=0)
    cend = C_CMP_STRIDE * lax.broadcasted_iota(jnp.int32, (1, nh), 1) + (C_CMP_LEN - 1)
    maskc = cend <= qpos
    distc = (qpos - cend).astype(F32)
    zq = jnp.zeros((rows, C_HD), F32)

    qg, slope, o_c, scores = [], [], [], []
    for g in range(C_KV_HEADS):
        heads = [g * C_REP + r for r in range(C_REP)]
        qg.append(jnp.concatenate([q_ref[:, h * C_HD:(h + 1) * C_HD] for h in heads], axis=0) * (C_HD ** -0.5))
        slope.append(jnp.concatenate([jnp.full((ts, 1), 2.0 ** -(h + 1), F32) for h in heads], axis=0))
        p = _msoftmax(_mm_nt(qg[g], cmpv[g]) - slope[g] * distc, maskc)
        o_c.append(_mm(p, cmpv[C_KV_HEADS + g]))
        pm = jnp.dot(p.astype(BF16), mcs_ref[...], preferred_element_type=F32)
        imp = pm[0:ts]
        for r in range(1, C_REP):
            imp = imp + pm[r * ts:(r + 1) * ts]
        scores.append(_block_scores(imp, qpos_t))
    pad = jnp.full((QB - C_KV_HEADS * ts, LANES), NEG, F32)
    st_ref[...] = jnp.concatenate(scores + [pad], axis=0).T
    sel = _topn_mask(st_ref, nblk)

    def softmax_pv(tiles, lanes):
        m = tiles[0][0].max(axis=-1, keepdims=True)
        for s, _, _ in tiles[1:]:
            m = jnp.maximum(m, s.max(axis=-1, keepdims=True))
        l = jnp.zeros((rows, 1), F32)
        acc = jnp.zeros((rows, KV_ROW), F32)
        for s, msk, kv in tiles:
            e = jnp.where(msk, jnp.exp(s - m), 0.0)
            l = l + jnp.sum(e, axis=-1, keepdims=True)
            acc = acc + _mm(e, kv)
        return acc[:, lanes] / jnp.maximum(l, 1e-30)

    kpos_new = q_start + lax.broadcasted_iota(jnp.int32, (1, ts), 1)
    outs = []
    for g in range(C_KV_HEADS):
        qz = jnp.concatenate([qg[g] if sl == g else zq for sl in range(2 * C_KV_HEADS)], axis=1)
        vl = slice((C_KV_HEADS + g) * C_HD, (C_KV_HEADS + g + 1) * C_HD)
        selg = jnp.concatenate([sel[g * ts:(g + 1) * ts]] * C_REP, axis=0)
        mx = jnp.dot(selg.astype(BF16), e_ref[...], preferred_element_type=F32)
        tiles = []
        for pg in range(n_pages):
            kv = sp[pg][...]
            kpos = pg * page + lax.broadcasted_iota(jnp.int32, (1, page), 1)
            dist = qpos - kpos
            msk = jnp.where(dist >= 0, mx[:, pg * page:(pg + 1) * page], 0.0) > 0.5
            tiles.append((jnp.where(msk, _mm_nt(qz, kv) - slope[g] * dist.astype(F32), NEG), msk, kv))
        kv = new_ref[:, KV_ROW:2 * KV_ROW]
        dist = qpos - kpos_new
        msk = jnp.where(dist >= 0, mx[:, n_pages * page:n_pages * page + ts], 0.0) > 0.5
        tiles.append((jnp.where(msk, _mm_nt(qz, kv) - slope[g] * dist.astype(F32), NEG), msk, kv))
        o_s = softmax_pv(tiles, vl)

        tiles = []
        for w0 in range(0, n_win, KT):
            kv = win_ref[w0:w0 + KT, :]
            kpos = (q_start - n_win + w0) + lax.broadcasted_iota(jnp.int32, (1, KT), 1)
            dist = qpos - kpos
            msk = dist.astype(jnp.uint32) < C_WINDOW
            tiles.append((jnp.where(msk, _mm_nt(qz, kv) - slope[g] * dist.astype(F32), NEG), msk, kv))
        kv = new_ref[:, 2 * KV_ROW:3 * KV_ROW]
        dist = qpos - kpos_new
        msk = dist.astype(jnp.uint32) < C_WINDOW
        tiles.append((jnp.where(msk, _mm_nt(qz, kv) - slope[g] * dist.astype(F32), NEG), msk, kv))
        o_w = softmax_pv(tiles, vl)

        gates = jax.nn.sigmoid(gate_ref[:, g * LANES:(g + 1) * LANES])
        for r in range(C_REP):
            rs = slice(r * ts, (r + 1) * ts)
            outs.append(gates[:, r:r + 1] * o_c[g][rs] + gates[:, C_REP + r:C_REP + r + 1] * o_s[rs]
                        + gates[:, 2 * C_REP + r:2 * C_REP + r + 1] * o_w[rs])
    o_ref[...] = jnp.concatenate(outs, axis=1)


def _nsa_sample(p, pool_cmp, pool_sel, win_buf, page_table, layer, cwx, nb, ts, past_len):
    wxa, wxb, pxa, pxb, w2 = cwx
    n_pages = page_table.shape[1]
    page = past_len // n_pages
    nh = past_len // C_CMP_STRIDE
    n_win = win_buf.shape[1]
    assert (past_len + ts) // C_CMP_STRIDE == nh and past_len % KT == 0 and n_win % KT == 0 and ts <= KT
    nblk = -(-(-(-(past_len + ts) // C_SEL_LEN)) // 8) * 8
    mcs = _cmp_to_sel(nh, LANES).astype(BF16)
    e = _block_expand(past_len + KT)
    cst = lambda a: pl.BlockSpec(a.shape, lambda b, pt: (0,) * a.ndim)
    half = KV_ROW // 2
    pg_spec = lambda pg: pl.BlockSpec((None, page, KV_ROW), lambda b, pt: (pt[b, pg], layer, 0))
    half_spec = lambda pg, kv: pl.BlockSpec((None, page, half), lambda b, pt: (pt[b, pg], layer, kv))
    page_specs = ([half_spec(pg, 0) for pg in range(n_pages)] + [half_spec(pg, 1) for pg in range(n_pages)]
                  + [pg_spec(pg) for pg in range(n_pages)])
    consts = (wxa, wxb, pxa, pxb, w2, mcs, e)
    grid_spec = pltpu.PrefetchScalarGridSpec(
        num_scalar_prefetch=1,
        grid=(nb,),
        in_specs=[pl.BlockSpec((ts, C_Q), lambda b, pt: (b, 0)),
                  pl.BlockSpec((ts, C_KV_HEADS * LANES), lambda b, pt: (b, OD_G // (C_KV_HEADS * LANES))),
                  pl.BlockSpec((ts, C_KV), lambda b, pt: (b, OD_KV // C_KV)),
                  pl.BlockSpec((None, n_win, KV_ROW), lambda b, pt: (b, 0, 0))]
                 + [cst(a) for a in consts] + page_specs,
        out_specs=pl.BlockSpec((ts, C_Q), lambda b, pt: (b, 0)),
        scratch_shapes=[pltpu.VMEM((LANES, QB), F32)],
    )
    return pl.pallas_call(
        functools.partial(_nsa_sample_kernel, n_pages=n_pages, page=page, ts=ts, q_start=past_len, nblk=nblk),
        grid_spec=grid_spec,
        out_shape=jax.ShapeDtypeStruct((nb * ts, C_Q), F32),
        compiler_params=_cparams(("parallel",)),
        name="nsa_sample",
    )(page_table, p, p, p, win_buf, *consts, *([pool_cmp] * (2 * n_pages)), *([pool_sel] * n_pages))


def _odd_mix_sample(x, g_pre, w_in, cw, dw, pool_cmp, pool_sel, win_buf, page_table, layer, nb, ts, past_len):
    _, rowwise = cw
    p = _in_proj(x, g_pre, w_in)
    od, dv = _chunk_mlp(p, *dw, nb * ts, ts)
    kv_new = p[:, OD_KV:OD_KV + C_KV]
    oc = _nsa_sample(p, pool_cmp, pool_sel, win_buf.reshape(nb, win_buf.shape[1], KV_ROW), page_table, layer,
                     rowwise, nb, ts, past_len)
    return oc, od, kv_new, dv


def kernel(x_prompt, x_sample, state_delta, state_conv_a, state_conv_b, cache_cmp_kv, cache_sel_kv, cache_win_kv, page_table, norm_g, ffn_gate, ffn_up, ffn_down, ev_w_in, ev_w_out, ev_a_conv, ev_a_log, ev_dt_bias, ev_a_norm, ev_b_conv, od_w_in, od_w_out, od_cmp_pe, od_cmp_w1, od_cmp_w2, od_d_ws, od_d_bs, od_d_ln_g, od_d_ln_b):
    depth = norm_g.shape[0]
    n_even = (depth + 1) // 2
    n_odd = depth // 2
    bf = lambda a: a.astype(BF16)
    bp, tp, _ = x_prompt.shape
    bs, ts, _ = x_sample.shape
    page = cache_cmp_kv.shape[2]
    n_pages = page_table.shape[1]
    past_len = n_pages * page
    n_pool = cache_cmp_kv.shape[0]
    pool_cmp = cache_cmp_kv.reshape(n_pool, n_odd * page, C_KV // 3)
    pool_sel = cache_sel_kv.reshape(n_pool, n_odd * page, C_KV // 3)

    xp = x_prompt.reshape(bp * tp, D_MODEL)
    xs = x_sample.reshape(bs * ts, D_MODEL)
    zeros = lambda *s: jnp.zeros(s, F32)
    res = {k: [] for k in ("dp", "ds", "cap", "cas", "cbp", "cbs", "cmpp", "cmps", "selp", "sels", "winp", "wins", "dvs")}
    for li in range(depth):
        j = li // 2
        g = norm_g[li]
        ffn1 = (bf(ffn_gate[li, 0]), bf(ffn_up[li, 0]), bf(ffn_down[li, 0]))
        ffn2 = (bf(ffn_gate[li, 1]), bf(ffn_up[li, 1]), bf(ffn_down[li, 1]))
        xp = _ffn_half(xp, g[0], g[1], *ffn1)
        xs = _ffn_half(xs, g[0], g[1], *ffn1)
        if li % 2 == 0:
            w_in = bf(_even_w_in(ev_w_in[j]))
            w_out = bf(ev_w_out[j])
            mixw = (ev_a_conv[j], ev_a_log[j], ev_dt_bias[j], ev_a_norm[j], ev_b_conv[j])
            oa, ob, s, ca, cb = _even_mix(xp, g[2], w_in, zeros(bp, A_HEADS, A_DK, A_DV), zeros(bp, A_CONV - 1, A_CONV_DIM),
                                          zeros(bp, B_CONV - 1, B_WIDTH), *mixw, bp, tp)
            xp = _out_proj(xp, oa, ob, g[3], w_out)
            res["dp"].append(s); res["cap"].append(ca); res["cbp"].append(cb)
            oa, ob, s, ca, cb = _even_mix(xs, g[2], w_in, state_delta[:, j], state_conv_a[:, j], state_conv_b[:, j],
                                          *mixw, bs, ts)
            xs = _out_proj(xs, oa, ob, g[3], w_out)
            res["ds"].append(s); res["cas"].append(ca); res["cbs"].append(cb)
        else:
            w_in = bf(_odd_w_in(od_w_in[j]))
            w_out = bf(od_w_out[j])
            cw = _compress_weights(od_cmp_pe[j], od_cmp_w1[j], od_cmp_w2[j])
            dw = (od_d_ws[j], od_d_bs[j], od_d_ln_g[j], od_d_ln_b[j])
            oc, od, kv_new, _ = _odd_mix_prompt(xp, g[2], w_in, cw, dw, bp, tp)
            xp = _out_proj(xp, oc, od, g[3], w_out)
            kv_new = kv_new.reshape(bp, tp, 3, 2, C_KV_HEADS, C_HD)
            res["cmpp"].append(kv_new[:, :, 0]); res["selp"].append(kv_new[:, :, 1])
            res["winp"].append(kv_new[:, tp - min(C_WINDOW, tp):, 2])
            oc, od, kv_new, dv = _odd_mix_sample(xs, g[2], w_in, cw, dw, pool_cmp, pool_sel, cache_win_kv[:, j],
                                                 page_table, j, bs, ts, past_len)
            xs = _out_proj(xs, oc, od, g[3], w_out)
            kv_new = kv_new.reshape(bs, ts, 3, 2, C_KV_HEADS, C_HD)
            res["cmps"].append(kv_new[:, :, 0]); res["sels"].append(kv_new[:, :, 1])
            win_all = jnp.concatenate([cache_win_kv[:, j], kv_new[:, :, 2]], axis=1)
            res["wins"].append(win_all[:, win_all.shape[1] - min(C_WINDOW, past_len + ts):])
            res["dvs"].append(dv.reshape(bs, ts, D_WIDTH))
        xp = _ffn_half(xp, g[4], g[5], *ffn2)
        xs = _ffn_half(xs, g[4], g[5], *ffn2)
    st = lambda k: jnp.stack(res[k], axis=1)
    return (xp.reshape(bp, tp, D_MODEL), xs.reshape(bs, ts, D_MODEL), st("dp"), st("ds"), st("cap"), st("cas"),
            st("cbp"), st("cbs"), st("cmpp"), st("cmps"), st("selp"), st("sels"), st("winp"), st("wins"), st("dvs"))
```
